```python
import jax, jax.numpy as jnp
from jax import lax
import numpy as np

D_MODEL = 1024
BATCH = 4
SEQ = 4096
DEPTH = 2
DEC_BATCH = 32
DEC_SEQ = 4
PAST_LEN = 8192
PAGE_SIZE = 128

SB_HEADS = 8
HEAD_DIM = 64
SB_WIDTH = SB_HEADS * HEAD_DIM
D_CONV = D_MODEL // 2
CONV_GROUPS = 8
CONV_K = 3
D_FF = 2816
N_EXPERTS = 8
TOP_K = 2
D_FF_EXPERT = 3584
Q_BLOCK = 128
N_DENSE = (DEPTH + 1) // 2
N_MOE = DEPTH // 2
RMS_EPS = 1e-6
SB_BIAS_INIT = -6.0
IN_SPLITS = [SB_WIDTH, SB_WIDTH, SB_WIDTH, D_CONV, D_CONV, D_CONV, D_MODEL, D_MODEL]
D_IN = sum(IN_SPLITS)
SB_SCALE = HEAD_DIM ** -0.5

kernel_name = "stickbreak_shortconv_gated_hybrid_step"


def rmsnorm(x, g):
    xf = x.astype(jnp.float32)
    y = xf * lax.rsqrt(jnp.mean(xf * xf, axis=-1, keepdims=True) + RMS_EPS)
    return (y * g.astype(jnp.float32)).astype(x.dtype)


def sb_weights(z, mask):
    log_one_minus = jnp.where(mask, -jax.nn.softplus(z), 0.0)
    excl = lax.cumsum(log_one_minus, axis=z.ndim - 1, reverse=True) - log_one_minus
    return jnp.where(mask, jnp.exp(jax.nn.log_sigmoid(z) + excl), 0.0)


def sb_prompt(q, k, v, bias):
    b, s, h, dh = q.shape
    nb = s // Q_BLOCK
    qb = q.reshape(b, nb, Q_BLOCK, h, dh).transpose(1, 0, 2, 3, 4)
    kpos = jnp.arange(s)
    bias_f = bias.astype(jnp.float32)[None, :, None, None]

    def block(args):
        qi, start = args
        z = jnp.einsum('bqhd,bkhd->bhqk', qi, k, preferred_element_type=jnp.float32) * SB_SCALE + bias_f
        qpos = start + jnp.arange(Q_BLOCK)
        mask = kpos[None, :] < qpos[:, None]
        w = sb_weights(z, mask)
        return jnp.einsum('bhqk,bkhd->bqhd', w.astype(v.dtype), v)

    o = lax.map(block, (qb, jnp.arange(nb) * Q_BLOCK))
    return o.transpose(1, 0, 2, 3, 4).reshape(b, s, h * dh)


def sb_sample(q, k_new, v_new, bias, k_past, v_past):
    b, t, h, dh = q.shape
    p = k_past.shape[1]
    bias_f = bias.astype(jnp.float32)[None, :, None, None]
    zp = jnp.einsum('bqhd,bkhd->bhqk', q, k_past, preferred_element_type=jnp.float32) * SB_SCALE
    zn = jnp.einsum('bqhd,bkhd->bhqk', q, k_new, preferred_element_type=jnp.float32) * SB_SCALE
    z = jnp.concatenate([zp, zn], axis=-1) + bias_f
    ar = jnp.arange(t)
    mask = jnp.concatenate([jnp.ones((t, p), dtype=bool), ar[None, :] < ar[:, None]], axis=1)
    w = sb_weights(z, mask).astype(v_new.dtype)
    o = (jnp.einsum('bhqk,bkhd->bqhd', w[..., :p], v_past)
         + jnp.einsum('bhqk,bkhd->bqhd', w[..., p:], v_new))
    return o.reshape(b, t, h * dh)


def short_conv(u, buf, w):
    t = u.shape[1]
    full = jnp.concatenate([buf, u], axis=1)
    y = w[0] * full[:, 0:t]
    for i in range(1, CONV_K):
        y = y + w[i] * full[:, i:i + t]
    return y, full[:, t:]


def mixer(xn, w_in, sb_bias, w_a, conv_w, w_b, w_o, attend, conv_buf):
    b, t, _ = xn.shape
    p = xn @ w_in
    q, k, v, bg, cg, hh, ga, gb = jnp.split(p, list(np.cumsum(IN_SPLITS)[:-1]), axis=-1)
    q = q.reshape(b, t, SB_HEADS, HEAD_DIM)
    k = k.reshape(b, t, SB_HEADS, HEAD_DIM)
    v = v.reshape(b, t, SB_HEADS, HEAD_DIM)
    ya = attend(q, k, v, sb_bias) @ w_a
    cv, new_buf = short_conv(cg * hh, conv_buf, conv_w)
    yb = (bg * cv) @ w_b
    y = (jax.nn.sigmoid(ga) * ya + jax.nn.sigmoid(gb) * yb) @ w_o
    return y, k, v, new_buf


def swiglu(x, wg, wu, wd):
    return (jax.nn.silu(x @ wg) * (x @ wu)) @ wd


def moe(x, router, wg, wu, wd):
    shp = x.shape
    xf = x.reshape(-1, shp[-1])
    logits = (xf @ router).astype(jnp.float32)
    topv, topi = lax.top_k(logits, TOP_K)
    gates = jax.nn.softmax(topv, axis=-1)
    comb = jnp.sum(jax.nn.one_hot(topi, N_EXPERTS, dtype=jnp.float32) * gates[..., None], axis=-2).astype(x.dtype)
    out = comb[:, 0:1] * swiglu(xf, wg[0], wu[0], wd[0])
    for e in range(1, N_EXPERTS):
        out = out + comb[:, e:e + 1] * swiglu(xf, wg[e], wu[e], wd[e])
    return out.reshape(shp)


def setup_inputs(seed: int = 0) -> dict:
    key = jax.random.key(seed)
    ks = jax.random.split(key, 24)
    n_pages = PAST_LEN // PAGE_SIZE
    n_used = DEC_BATCH * n_pages
    n_pool = (n_used * 5) // 4
    nrm = jax.random.normal
    f32 = jnp.float32
    perm = jax.random.permutation(ks[5], n_pool)[:n_used]
    return {
        "x_prompt": nrm(ks[0], (BATCH, SEQ, D_MODEL), f32),
        "x_sample": nrm(ks[1], (DEC_BATCH, DEC_SEQ, D_MODEL), f32),
        "cache_k": nrm(ks[2], (DEPTH, n_pool, PAGE_SIZE, SB_HEADS, HEAD_DIM), f32),
        "cache_v": nrm(ks[3], (DEPTH, n_pool, PAGE_SIZE, SB_HEADS, HEAD_DIM), f32),
        "state_conv": nrm(ks[4], (DEPTH, DEC_BATCH, CONV_K - 1, D_CONV), f32),
        "page_table": perm.reshape(DEC_BATCH, n_pages).astype(jnp.int32),
        "norm_mix": 1.0 + 0.02 * nrm(ks[6], (DEPTH, D_MODEL), f32),
        "w_in": nrm(ks[7], (DEPTH, D_MODEL, D_IN), f32) * D_MODEL ** -0.5,
        "sb_bias": SB_BIAS_INIT + 0.1 * nrm(ks[21], (DEPTH, SB_HEADS), f32),
        "w_a": nrm(ks[8], (DEPTH, SB_WIDTH, D_MODEL), f32) * SB_WIDTH ** -0.5,
        "conv_w": nrm(ks[9], (DEPTH, CONV_K, D_CONV), f32) * CONV_K ** -0.5,
        "w_b": nrm(ks[10], (DEPTH, D_CONV, D_MODEL), f32) * D_CONV ** -0.5,
        "w_o": nrm(ks[11], (DEPTH, D_MODEL, D_MODEL), f32) * D_MODEL ** -0.5,
        "norm_ffn": 1.0 + 0.02 * nrm(ks[12], (DEPTH, D_MODEL), f32),
        "ffn_wg": nrm(ks[13], (N_DENSE, D_MODEL, D_FF), f32) * D_MODEL ** -0.5,
        "ffn_wu": nrm(ks[14], (N_DENSE, D_MODEL, D_FF), f32) * D_MODEL ** -0.5,
        "ffn_wd": nrm(ks[15], (N_DENSE, D_FF, D_MODEL), f32) * D_FF ** -0.5,
        "router": nrm(ks[16], (N_MOE, D_MODEL, N_EXPERTS), f32) * D_MODEL ** -0.5,
        "moe_wg": nrm(ks[17], (N_MOE, N_EXPERTS, D_MODEL, D_FF_EXPERT), f32) * D_MODEL ** -0.5,
        "moe_wu": nrm(ks[18], (N_MOE, N_EXPERTS, D_MODEL, D_FF_EXPERT), f32) * D_MODEL ** -0.5,
        "moe_wd": nrm(ks[19], (N_MOE, N_EXPERTS, D_FF_EXPERT, D_MODEL), f32) * D_FF_EXPERT ** -0.5,
        "norm_final": 1.0 + 0.02 * nrm(ks[20], (D_MODEL,), f32),
    }


def reference(x_prompt, x_sample, cache_k, cache_v, state_conv, page_table,
              norm_mix, w_in, sb_bias, w_a, conv_w, w_b, w_o, norm_ffn,
              ffn_wg, ffn_wu, ffn_wd, router, moe_wg, moe_wu, moe_wd, norm_final):
    db, n_pages = page_table.shape
    past = n_pages * PAGE_SIZE
    xp, xs = x_prompt, x_sample
    kp_l, vp_l, cp_l, ks_l, vs_l, cs_l = [], [], [], [], [], []
    for l in range(DEPTH):
        zero_buf = jnp.zeros((xp.shape[0], CONV_K - 1, D_CONV), xp.dtype)
        hp, kp, vp, bp = mixer(rmsnorm(xp, norm_mix[l]), w_in[l], sb_bias[l], w_a[l], conv_w[l], w_b[l], w_o[l],
                               sb_prompt, zero_buf)
        k_past = cache_k[l][page_table].reshape(db, past, SB_HEADS, HEAD_DIM)
        v_past = cache_v[l][page_table].reshape(db, past, SB_HEADS, HEAD_DIM)
        hs, ksn, vsn, bs = mixer(rmsnorm(xs, norm_mix[l]), w_in[l], sb_bias[l], w_a[l], conv_w[l], w_b[l], w_o[l],
                                 lambda q, k, v, bias: sb_sample(q, k, v, bias, k_past, v_past), state_conv[l])
        xp = xp + hp
        xs = xs + hs
        fi = l // 2
        if l % 2 == 0:
            xp = xp + swiglu(rmsnorm(xp, norm_ffn[l]), ffn_wg[fi], ffn_wu[fi], ffn_wd[fi])
            xs = xs + swiglu(rmsnorm(xs, norm_ffn[l]), ffn_wg[fi], ffn_wu[fi], ffn_wd[fi])
        else:
            xp = xp + moe(rmsnorm(xp, norm_ffn[l]), router[fi], moe_wg[fi], moe_wu[fi], moe_wd[fi])
            xs = xs + moe(rmsnorm(xs, norm_ffn[l]), router[fi], moe_wg[fi], moe_wu[fi], moe_wd[fi])
        kp_l.append(kp); vp_l.append(vp); cp_l.append(bp)
        ks_l.append(ksn); vs_l.append(vsn); cs_l.append(bs)
    y_prompt = rmsnorm(xp, norm_final)
    y_sample = rmsnorm(xs, norm_final)
    new_k_prompt = jnp.stack(kp_l)
    new_v_prompt = jnp.stack(vp_l)
    new_conv_prompt = jnp.stack(cp_l)
    new_k_sample = jnp.stack(ks_l)
    new_v_sample = jnp.stack(vs_l)
    new_conv_sample = jnp.stack(cs_l)
    return (y_prompt, y_sample, new_k_prompt, new_v_prompt, new_conv_prompt, new_k_sample, new_v_sample, new_conv_sample)
```

```python
import functools

import jax
import jax.numpy as jnp
from jax import lax
from jax.experimental import pallas as pl
from jax.experimental.pallas import tpu as pltpu

F32 = jnp.float32
BF16 = jnp.bfloat16
I32 = jnp.int32

RMS_EPS = 1e-6
TOP_K = 2
LANES = 128
SUBLANES = 8
VMEM_LIMIT = 56 * 1024 * 1024


def _pick(n, cands):
    for c in cands:
        if n % c == 0:
            return c
    return n


def _params(sem):
    return pltpu.CompilerParams(dimension_semantics=sem, vmem_limit_bytes=VMEM_LIMIT)


def _const_spec(shape):
    nd = len(shape)
    return pl.BlockSpec(shape, lambda *_: (0,) * nd, pipeline_mode=pl.Buffered(1))


def _rms(x, g):
    return x * lax.rsqrt(jnp.mean(x * x, axis=-1, keepdims=True) + RMS_EPS) * g


def _sigmoid(x):
    return 1.0 / (1.0 + jnp.exp(-x))


def _softplus(z):
    return jnp.maximum(z, 0.0) + jnp.log(1.0 + jnp.exp(-jnp.abs(z)))


def _in_proj_kernel(x_ref, g_ref, w_ref, wkv_ref, q_ref, kt_ref, vt_ref, ktb_ref, vtb_ref, bg_ref, u_ref,
                    sga_ref, sgb_ref, *, sbw, dc, d, scale):
    xn = _rms(x_ref[...], g_ref[...]).astype(BF16)

    def proj(lo, n):
        return jnp.dot(xn, w_ref[:, lo:lo + n], preferred_element_type=F32)

    def proj_t(lo):
        return lax.dot_general(wkv_ref[lo:lo + sbw, :], xn, (((1,), (1,)), ((), ())),
                               preferred_element_type=F32)

    q_ref[...] = (proj(0, sbw) * scale).astype(BF16)
    kt = proj_t(0)
    kt_ref[...] = kt
    ktb_ref[...] = kt.astype(BF16)
    vt = proj_t(sbw)
    vt_ref[...] = vt
    vtb_ref[...] = vt.astype(BF16)
    o = 3 * sbw
    bg_ref[...] = proj(o, dc).astype(BF16)
    u_ref[...] = proj(o + dc, dc) * proj(o + 2 * dc, dc)
    o += 3 * dc
    sga_ref[...] = _sigmoid(proj(o, d)).astype(BF16)
    sgb_ref[...] = _sigmoid(proj(o + d, d)).astype(BF16)


def _in_proj(x, g, w, wkv_t, b, s, sbw, dc, scale):
    n, d = x.shape
    tm = _pick(s, (512, 256, 128))
    nt = s // tm
    row = lambda c: pl.BlockSpec((tm, c), lambda i: (i, 0))
    col = lambda: pl.BlockSpec((None, sbw, tm), lambda i: (i // nt, 0, i % nt))
    outs = [(row(sbw), (n, sbw), BF16), (col(), (b, sbw, s), F32), (col(), (b, sbw, s), F32),
            (col(), (b, sbw, s), BF16), (col(), (b, sbw, s), BF16), (row(dc), (n, dc), BF16),
            (row(dc), (n, dc), F32), (row(d), (n, d), BF16), (row(d), (n, d), BF16)]
    return pl.pallas_call(
        functools.partial(_in_proj_kernel, sbw=sbw, dc=dc, d=d, scale=scale),
        grid=(n // tm,),
        in_specs=[row(d), _const_spec((1, d)), _const_spec(w.shape), _const_spec(wkv_t.shape)],
        out_specs=[o[0] for o in outs],
        out_shape=[jax.ShapeDtypeStruct(o[1], o[2]) for o in outs],
        compiler_params=_params(("parallel",)),
        name="in_proj",
    )(x, g, w, wkv_t)


def _sb_block(q, ktj, vtj, nt, bias, c, acc, mask):
    z = jnp.dot(q, ktj, preferred_element_type=F32) + bias
    sp = _softplus(z)
    if mask is not None:
        sp = jnp.where(mask, sp, 0.0)
    e = jnp.dot(sp.astype(BF16), nt, preferred_element_type=F32)
    aw = jnp.exp(z - sp + e + c)
    if mask is not None:
        aw = jnp.where(mask, aw, 0.0)
    acc = acc + lax.dot_general(aw.astype(BF16), vtj, (((1,), (1,)), ((), ())), preferred_element_type=F32)
    c = c + e[:, :1] - sp[:, :1]
    return c, acc


def _sb_prompt_kernel(bias_ref, q_ref, kt_ref, vt_ref, nt_ref, o_ref, *, tq, hd, hpg):
    g = pl.program_id(1)
    i = pl.program_id(2)
    w = hpg * hd
    q2 = q_ref[...].astype(F32)
    nt = nt_ref[...]
    lane = lax.broadcasted_iota(I32, (tq, w), 1)
    row = lax.broadcasted_iota(I32, (tq, tq), 0)
    col = lax.broadcasted_iota(I32, (tq, tq), 1)
    causal = col < row
    out = jnp.zeros((tq, w), F32)
    for a in range(hpg):
        in_head = (lane >= a * hd) & (lane < (a + 1) * hd)
        qa = jnp.where(in_head, q2, 0.0).astype(BF16)
        bias = bias_ref[g * hpg + a]

        def blk(j, c, acc, mask):
            ktj = kt_ref[:, pl.ds(pl.multiple_of(j * tq, tq), tq)]
            vtj = vt_ref[:, pl.ds(pl.multiple_of(j * tq, tq), tq)]
            return _sb_block(qa, ktj, vtj, nt, bias, c, acc, mask)

        c, acc = blk(i, jnp.zeros((tq, 1), F32), jnp.zeros((tq, w), F32), causal)
        c, acc = lax.fori_loop(0, i, lambda jj, ca: blk(i - 1 - jj, ca[0], ca[1], None), (c, acc))
        out = jnp.where(in_head, acc, out)
    o_ref[...] = out.astype(BF16)


def _neg_tri(t):
    j = lax.broadcasted_iota(I32, (t, t), 0)
    s = lax.broadcasted_iota(I32, (t, t), 1)
    return jnp.where(j > s, -1.0, 0.0).astype(BF16)


def _sb_prompt(q, ktb, vtb, bias, hd):
    b, sbw, s = ktb.shape
    n = q.shape[0]
    hpg = max(1, LANES // hd)
    w = hpg * hd
    tq = _pick(s, (256, 128))
    nq = s // tq
    return pl.pallas_call(
        functools.partial(_sb_prompt_kernel, tq=tq, hd=hd, hpg=hpg),
        grid=(b, sbw // w, nq),
        in_specs=[pl.BlockSpec(memory_space=pltpu.SMEM),
                  pl.BlockSpec((tq, w), lambda bi, g, i: (bi * nq + i, g)),
                  pl.BlockSpec((None, w, s), lambda bi, g, i: (bi, g, 0)),
                  pl.BlockSpec((None, w, s), lambda bi, g, i: (bi, g, 0)),
                  _const_spec((tq, tq))],
        out_specs=pl.BlockSpec((tq, w), lambda bi, g, i: (bi * nq + i, g)),
        out_shape=jax.ShapeDtypeStruct((n, sbw), BF16),
        compiler_params=_params(("parallel", "parallel", "arbitrary")),
        name="sb_prompt",
    )(bias, q, ktb, vtb, _neg_tri(tq))


def _sb_sample_kernel(pt_ref, q_ref, bias_ref, kn_ref, vn_ref, ntn_ref, nt_ref, *rest, pg, t, h, hd, rows):
    k_refs = rest[:pg]
    v_refs = rest[pg:2 * pg]
    o_ref = rest[2 * pg]
    c_ref, acc_ref = rest[2 * pg + 1:]
    bi = pl.program_id(0)
    s = pl.program_id(1)
    sbw = h * hd
    bias = bias_ref[...]
    q = q_ref[...].astype(F32)
    qrows = jnp.concatenate(
        [jnp.broadcast_to(q[i:i + 1], (h, sbw)) for i in range(t)]
        + [jnp.zeros((rows - t * h, sbw), q.dtype)], axis=0)
    r = lax.broadcasted_iota(I32, (rows, sbw), 0)
    cidx = lax.broadcasted_iota(I32, (rows, sbw), 1)
    headmask = (cidx // hd) == (r % h)
    qbd = jnp.where(headmask, qrows, 0.0).astype(BF16)

    def step(kt, vt, nt, mask):
        c, acc = _sb_block(qbd, kt.astype(BF16), vt.astype(BF16), nt, bias, c_ref[...], acc_ref[...], mask)
        c_ref[...] = c
        acc_ref[...] = acc

    @pl.when(s == 0)
    def _():
        c_ref[...] = jnp.zeros_like(c_ref)
        acc_ref[...] = jnp.zeros_like(acc_ref)
        ns = kn_ref.shape[1]
        rq = lax.broadcasted_iota(I32, (rows, ns), 0) // h
        ck = lax.broadcasted_iota(I32, (rows, ns), 1)
        mask = (ck // t == bi) & (ck % t < rq)
        step(kn_ref[...], vn_ref[...], ntn_ref[...], mask)

    nt = nt_ref[...]
    page = nt.shape[0]
    for i in reversed(range(pg)):
        step(k_refs[i][...].reshape(sbw, page), v_refs[i][...].reshape(sbw, page), nt, None)

    @pl.when(s == pl.num_programs(1) - 1)
    def _():
        acc = jnp.where(headmask, acc_ref[...], 0.0)
        o_ref[...] = jnp.sum(acc[:t * h].reshape(t, h, sbw), axis=1).astype(BF16)


def _sb_sample(q, kt_new, vt_new, bias, cache_kt, cache_vt, layer, page_table, hd):
    db, t, sbw = q.shape
    h = sbw // hd
    page = cache_kt.shape[4]
    ns = kt_new.shape[1]
    n_pages = page_table.shape[1]
    pg = _pick(n_pages, (8, 4, 2, 1))
    n_steps = n_pages // pg
    rows = -(-t * h // LANES) * LANES
    bias_rows = jnp.pad(jnp.tile(bias, t), (0, rows - t * h)).reshape(rows, 1)

    def page_spec(i):
        def imap(bi, s, pt):
            return (layer, pt[bi * n_pages + (n_steps - 1 - s) * pg + i], 0, 0, 0)
        return pl.BlockSpec((None, None, h, hd, page), imap)

    const = lambda shape: pl.BlockSpec(shape, lambda bi, s, pt: (0,) * len(shape))
    qspec = pl.BlockSpec((None, t, sbw), lambda bi, s, pt: (bi, 0, 0))
    gs = pltpu.PrefetchScalarGridSpec(
        num_scalar_prefetch=1,
        grid=(db, n_steps),
        in_specs=[qspec, const((rows, 1)), const((sbw, ns)), const((sbw, ns)), const((ns, ns)),
                  const((page, page))] + [page_spec(i) for i in range(pg)] * 2,
        out_specs=qspec,
        scratch_shapes=[pltpu.VMEM((rows, 1), F32), pltpu.VMEM((rows, sbw), F32)],
    )
    return pl.pallas_call(
        functools.partial(_sb_sample_kernel, pg=pg, t=t, h=h, hd=hd, rows=rows),
        grid_spec=gs,
        out_shape=jax.ShapeDtypeStruct((db, t, sbw), BF16),
        compiler_params=_params(("parallel", "arbitrary")),
        name="sb_sample",
    )(page_table.reshape(-1), q, bias_rows, kt_new, vt_new, _neg_tri(ns), _neg_tri(page),
      *([cache_kt] * pg), *([cache_vt] * pg))


def _mix_out_kernel(*refs, seq, period, mode, final_norm, n_exp):
    it = iter(refs)
    x_ref, o_ref, bg_ref, u_ref = next(it), next(it), next(it), next(it)
    if period is None:
        uprev_ref = next(it)
    else:
        h1_ref, h2_ref = next(it), next(it)
    sga_ref, sgb_ref, cw_ref, wa_ref, wb_ref, wo_ref, gf_ref = (next(it) for _ in range(7))
    if mode == "dense":
        wg_ref, wu_ref, wd_ref = next(it), next(it), next(it)
        if final_norm:
            gn_ref = next(it)
        y_ref = next(it)
    else:
        rt_ref, tri_ref = next(it), next(it)
        x1_ref, xn_ref, ri_ref, rg_ref, cnt_ref = (next(it) for _ in range(5))

    tm = x_ref.shape[0]
    u = u_ref[...]
    row = lax.broadcasted_iota(I32, u.shape, 0)
    u1 = pltpu.roll(u, 1, axis=0)
    u2 = pltpu.roll(u, 2, axis=0)
    if period is None:
        first = (pl.program_id(0) % (seq // tm)) == 0
        keep = jnp.where(first, 0.0, 1.0)
        hm1 = uprev_ref[SUBLANES - 1:SUBLANES, :] * keep
        hm2 = uprev_ref[SUBLANES - 2:SUBLANES - 1, :] * keep
        u1 = jnp.where(row == 0, hm1, u1)
        u2 = jnp.where(row == 0, hm2, jnp.where(row == 1, hm1, u2))
    else:
        u1 = jnp.where(row % period == 0, h1_ref[...], u1)
        u2 = jnp.where(row % period < 2, h2_ref[...], u2)
    cv = cw_ref[0:1, :] * u2 + cw_ref[1:2, :] * u1 + cw_ref[2:3, :] * u

    ya = jnp.dot(o_ref[...], wa_ref[...], preferred_element_type=F32)
    yb = jnp.dot((bg_ref[...].astype(F32) * cv).astype(BF16), wb_ref[...], preferred_element_type=F32)
    mix = (sga_ref[...].astype(F32) * ya + sgb_ref[...].astype(F32) * yb).astype(BF16)
    x1 = x_ref[...] + jnp.dot(mix, wo_ref[...], preferred_element_type=F32)
    xnf = _rms(x1, gf_ref[...])
    xn = xnf.astype(BF16)

    if mode == "dense":
        hg = jnp.dot(xn, wg_ref[...], preferred_element_type=F32)
        hu = jnp.dot(xn, wu_ref[...], preferred_element_type=F32)
        hh = (hg * _sigmoid(hg) * hu).astype(BF16)
        y = x1 + jnp.dot(hh, wd_ref[...], preferred_element_type=F32)
        if final_norm:
            y = _rms(y, gn_ref[...])
        y_ref[...] = y
        return

    x1_ref[...] = x1
    xn_ref[...] = xn
    ep = rt_ref.shape[0]
    logits = lax.dot_general(rt_ref[...], xnf, (((1,), (1,)), ((), ())),
                             precision=lax.Precision.HIGHEST, preferred_element_type=F32)
    ie = lax.broadcasted_iota(I32, (ep, tm), 0)
    logits = jnp.where(ie < n_exp, logits, -jnp.inf)
    m1 = jnp.max(logits, axis=0, keepdims=True)
    i1 = jnp.min(jnp.where(logits == m1, ie, ep), axis=0, keepdims=True)
    l2 = jnp.where(ie == i1, -jnp.inf, logits)
    m2 = jnp.max(l2, axis=0, keepdims=True)
    i2 = jnp.min(jnp.where(l2 == m2, ie, ep), axis=0, keepdims=True)
    e2 = jnp.exp(m2 - m1)
    g1 = 1.0 / (1.0 + e2)
    g2 = e2 / (1.0 + e2)
    oh = jnp.where((ie == i1) | (ie == i2), 1.0, 0.0)
    before = jnp.dot(oh.astype(BF16), tri_ref[...], preferred_element_type=F32)
    w1 = jnp.sum(jnp.where(ie == i1, before, 0.0), axis=0, keepdims=True).astype(I32)
    w2 = jnp.sum(jnp.where(ie == i2, before, 0.0), axis=0, keepdims=True).astype(I32)
    r8 = lax.broadcasted_iota(I32, (SUBLANES, tm), 0)
    ri_ref[...] = jnp.where(r8 == 0, i1, jnp.where(r8 == 1, i2, jnp.where(r8 == 2, w1, jnp.where(r8 == 3, w2, 0))))
    rg_ref[...] = jnp.where(r8 == 0, g1, jnp.where(r8 == 1, g2, 0.0))
    cnt_ref[...] = jnp.broadcast_to(jnp.sum(oh, axis=1, keepdims=True), (ep, LANES))


def _mix_out(x, o, bg, u, sga, sgb, cw, wa, wb, wo, gf, *, seq, hist, mode, ffn, final_g, tm):
    n, d = x.shape
    sbw, dc = o.shape[1], bg.shape[1]
    nt = n // tm
    row = lambda c: pl.BlockSpec((tm, c), lambda i: (i, 0))
    ins = [x, o, bg, u]
    specs = [row(d), row(sbw), row(dc), row(dc)]
    if hist is None:
        assert seq % tm == 0 and tm % SUBLANES == 0
        period = None
        ins.append(u)
        specs.append(pl.BlockSpec((SUBLANES, dc), lambda i: (jnp.maximum(i * (tm // SUBLANES) - 1, 0), 0)))
    else:
        assert tm % seq == 0
        period = seq
        ins += list(hist)
        specs += [row(dc), row(dc)]
    cwp = jnp.pad(cw, ((0, SUBLANES - cw.shape[0]), (0, 0)))
    ins += [sga, sgb, cwp, wa, wb, wo, gf]
    specs += [row(d), row(d)] + [_const_spec(a.shape) for a in (cwp, wa, wb, wo, gf)]
    if mode == "dense":
        ins += list(ffn)
        specs += [_const_spec(a.shape) for a in ffn]
        if final_g is not None:
            ins.append(final_g)
            specs.append(_const_spec(final_g.shape))
        out_specs = row(d)
        out_shape = jax.ShapeDtypeStruct((n, d), F32)
        n_exp = 0
    else:
        router = ffn
        n_exp = router.shape[1]
        ep = -(-n_exp // SUBLANES) * SUBLANES
        rt = jnp.pad(router.T, ((0, ep - n_exp), (0, 0)))
        ti = lax.broadcasted_iota(I32, (tm, tm), 0)
        tj = lax.broadcasted_iota(I32, (tm, tm), 1)
        tri = jnp.where(ti < tj, 1.0, 0.0).astype(BF16)
        ins += [rt, tri]
        specs += [_const_spec(rt.shape), _const_spec(tri.shape)]
        lane_rows = lambda: pl.BlockSpec((SUBLANES, tm), lambda i: (0, i))
        out_specs = [row(d), row(d), lane_rows(), lane_rows(), pl.BlockSpec((None, ep, LANES), lambda i: (i, 0, 0))]
        out_shape = [jax.ShapeDtypeStruct((n, d), F32), jax.ShapeDtypeStruct((n, d), BF16),
                     jax.ShapeDtypeStruct((SUBLANES, n), I32), jax.ShapeDtypeStruct((SUBLANES, n), F32),
                     jax.ShapeDtypeStruct((nt, ep, LANES), F32)]
    return pl.pallas_call(
        functools.partial(_mix_out_kernel, seq=seq, period=period, mode=mode,
                          final_norm=final_g is not None, n_exp=n_exp),
        grid=(nt,),
        in_specs=specs,
        out_specs=out_specs,
        out_shape=out_shape,
        compiler_params=_params(("parallel",)),
        name="mix_out_" + mode,
    )(*ins)


def _dispatch_kernel(ii_ref, ij_ref, fl_ref, n_ref, x_ref, pos_ref, rg_ref, xs_ref, gs_ref, *, ts):
    w = pl.program_id(0)

    @pl.when(w < n_ref[0])
    def _():
        tt = x_ref.shape[0]
        r = ii_ref[w] * ts + lax.broadcasted_iota(I32, (ts, tt), 0)
        m1 = r == pos_ref[0:1, :]
        m2 = r == pos_ref[1:2, :]
        sel = jnp.where(m1 | m2, 1.0, 0.0).astype(BF16)
        xs = jnp.dot(sel, x_ref[...], preferred_element_type=F32).astype(BF16)
        gate = jnp.sum(jnp.where(m1, rg_ref[0:1, :], 0.0) + jnp.where(m2, rg_ref[1:2, :], 0.0),
                       axis=1, keepdims=True)
        gate = jnp.broadcast_to(gate, gs_ref.shape)

        @pl.when(fl_ref[w] == 1)
        def _():
            xs_ref[...] = xs
            gs_ref[...] = gate

        @pl.when(fl_ref[w] == 0)
        def _():
            xs_ref[...] += xs
            gs_ref[...] += gate


def _dispatch(xn, pos, rg, items, n_items, ts, tt, rows):
    n, d = xn.shape
    ii, ij, first = items
    gs = pltpu.PrefetchScalarGridSpec(
        num_scalar_prefetch=4,
        grid=(ii.shape[0],),
        in_specs=[pl.BlockSpec((tt, d), lambda w, ii, ij, fl, nn: (ij[w], 0)),
                  pl.BlockSpec((SUBLANES, tt), lambda w, ii, ij, fl, nn: (0, ij[w])),
                  pl.BlockSpec((SUBLANES, tt), lambda w, ii, ij, fl, nn: (0, ij[w]))],
        out_specs=[pl.BlockSpec((ts, d), lambda w, ii, ij, fl, nn: (ii[w], 0)),
                   pl.BlockSpec((ts, LANES), lambda w, ii, ij, fl, nn: (ii[w], 0))],
    )
    return pl.pallas_call(
        functools.partial(_dispatch_kernel, ts=ts),
        grid_spec=gs,
        out_shape=[jax.ShapeDtypeStruct((rows, d), BF16), jax.ShapeDtypeStruct((rows, LANES), F32)],
        compiler_params=_params(("arbitrary",)),
        name="moe_dispatch",
    )(ii, ij, first, n_items, xn, pos, rg)


def _experts_kernel(te_ref, nv_ref, xs_ref, wg_ref, wu_ref, wd_ref, gs_ref, y_ref, acc_ref):
    i = pl.program_id(0)
    c = pl.program_id(1)

    @pl.when(i < nv_ref[0])
    def _():
        x = xs_ref[...]
        hg = jnp.dot(x, wg_ref[...], preferred_element_type=F32)
        hu = jnp.dot(x, wu_ref[...], preferred_element_type=F32)
        hh = (hg * _sigmoid(hg) * hu).astype(BF16)
        part = jnp.dot(hh, wd_ref[...], preferred_element_type=F32)

        @pl.when(c == 0)
        def _():
            acc_ref[...] = part

        @pl.when(c > 0)
        def _():
            acc_ref[...] += part

        @pl.when(c == pl.num_programs(1) - 1)
        def _():
            y_ref[...] = (acc_ref[...] * gs_ref[:, 0:1]).astype(BF16)


def _experts(xs, gsort, wg, wu, wd, tile_expert, n_valid, ts):
    rows, d = xs.shape
    dff = wg.shape[2]
    fc = _pick(dff, (512, 256, 128))
    nfc = dff // fc
    tile = lambda i, nv: jnp.minimum(i, nv[0] - 1)
    chunk = lambda i, c, nv: jnp.where(i < nv[0], c, nfc - 1)
    gs = pltpu.PrefetchScalarGridSpec(
        num_scalar_prefetch=2,
        grid=(rows // ts, nfc),
        in_specs=[pl.BlockSpec((ts, d), lambda i, c, te, nv: (tile(i, nv), 0)),
                  pl.BlockSpec((None, d, fc), lambda i, c, te, nv: (te[i], 0, chunk(i, c, nv))),
                  pl.BlockSpec((None, d, fc), lambda i, c, te, nv: (te[i], 0, chunk(i, c, nv))),
                  pl.BlockSpec((None, fc, d), lambda i, c, te, nv: (te[i], chunk(i, c, nv), 0)),
                  pl.BlockSpec((ts, LANES), lambda i, c, te, nv: (tile(i, nv), 0))],
        out_specs=pl.BlockSpec((ts, d), lambda i, c, te, nv: (tile(i, nv), 0)),
        scratch_shapes=[pltpu.VMEM((ts, d), F32)],
    )
    return pl.pallas_call(
        _experts_kernel,
        grid_spec=gs,
        out_shape=jax.ShapeDtypeStruct((rows, d), BF16),
        compiler_params=_params(("arbitrary", "arbitrary")),
        name="moe_experts",
    )(tile_expert, n_valid, xs, wg, wu, wd, gsort)


def _combine_kernel(ii_ref, ij_ref, fl_ref, n_ref, x1_ref, ys_ref, posc_ref, gn_ref, y_ref, *, ts, final_norm):
    w = pl.program_id(0)

    @pl.when(w < n_ref[0])
    def _():
        tt = x1_ref.shape[0]
        r = ii_ref[w] * ts + lax.broadcasted_iota(I32, (tt, ts), 1)
        sel = jnp.where((r == posc_ref[:, 0:1]) | (r == posc_ref[:, 1:2]), 1.0, 0.0).astype(BF16)
        part = jnp.dot(sel, ys_ref[...], preferred_element_type=F32)
        fl = fl_ref[w]

        @pl.when(fl % 2 == 1)
        def _():
            y_ref[...] = x1_ref[...] + part

        @pl.when(fl % 2 == 0)
        def _():
            y_ref[...] += part

        if final_norm:
            @pl.when(fl >= 2)
            def _():
                y_ref[...] = _rms(y_ref[...], gn_ref[...])


def _combine(x1, ys, posc, items, n_items, final_g, ts, tt):
    n, d = x1.shape
    ii, ij, flags = items
    gn = final_g if final_g is not None else jnp.ones((1, d), F32)
    gs = pltpu.PrefetchScalarGridSpec(
        num_scalar_prefetch=4,
        grid=(ii.shape[0],),
        in_specs=[pl.BlockSpec((tt, d), lambda w, ii, ij, fl, nn: (ij[w], 0)),
                  pl.BlockSpec((ts, d), lambda w, ii, ij, fl, nn: (ii[w], 0)),
                  pl.BlockSpec((tt, LANES), lambda w, ii, ij, fl, nn: (ij[w], 0)),
                  pl.BlockSpec((1, d), lambda w, ii, ij, fl, nn: (0, 0))],
        out_specs=pl.BlockSpec((tt, d), lambda w, ii, ij, fl, nn: (ij[w], 0)),
    )
    return pl.pallas_call(
        functools.partial(_combine_kernel, ts=ts, final_norm=final_g is not None),
        grid_spec=gs,
        out_shape=jax.ShapeDtypeStruct((n, d), F32),
        compiler_params=_params(("arbitrary",)),
        name="moe_combine",
    )(ii, ij, flags, n_items, x1, ys, posc, gn)


def _route_tables(ri, cnt, n_exp, tt, ts):
    n = ri.shape[1]
    nj = n // tt
    cnt = cnt[:, :n_exp, 0].astype(I32)
    tot = jnp.sum(cnt, axis=0)
    gsize = (tot + ts - 1) // ts * ts
    gend = jnp.cumsum(gsize)
    goff = gend - gsize
    seg_start = goff[None, :] + jnp.cumsum(cnt, axis=0) - cnt
    e1, e2, w1, w2 = ri[0], ri[1], ri[2], ri[3]
    oh = lambda e: e[:, None] == jnp.arange(n_exp, dtype=I32)[None, :]
    base = jnp.repeat(seg_start, tt, axis=0)
    pos1 = jnp.sum(jnp.where(oh(e1), base, 0), axis=1) + w1
    pos2 = jnp.sum(jnp.where(oh(e2), base, 0), axis=1) + w2
    pos = jnp.zeros((SUBLANES, n), I32).at[0].set(pos1).at[1].set(pos2)
    posc = jnp.zeros((n, LANES), I32).at[:, 0].set(pos1).at[:, 1].set(pos2)

    valid = cnt > 0
    i0 = seg_start // ts
    i1 = (seg_start + cnt - 1) // ts
    jj = jnp.broadcast_to(jnp.arange(nj, dtype=I32)[:, None], cnt.shape)
    it_i = jnp.concatenate([i0.reshape(-1), i1.reshape(-1)])
    it_j = jnp.concatenate([jj.reshape(-1), jj.reshape(-1)])
    it_v = jnp.concatenate([valid.reshape(-1), (valid & (i1 > i0)).reshape(-1)])
    n_items = jnp.sum(it_v).astype(I32).reshape(1)
    big = jnp.iinfo(jnp.int32).max
    ni = n * TOP_K // ts + n_exp

    def ordered(key):
        order = jnp.argsort(jnp.where(it_v, key, big))
        last = jnp.maximum(n_items[0] - 1, 0)
        idx = jnp.minimum(jnp.arange(order.shape[0]), last)
        si, sj = it_i[order][idx], it_j[order][idx]
        live = jnp.arange(order.shape[0]) < n_items[0]
        return si, sj, live

    si, sj, live = ordered(it_i * nj + it_j)
    first = jnp.concatenate([jnp.ones((1,), bool), si[1:] != si[:-1]]) & live
    disp = (si, sj, first.astype(I32))
    ci, cj, live = ordered(it_j * ni + it_i)
    cfirst = jnp.concatenate([jnp.ones((1,), bool), cj[1:] != cj[:-1]]) & live
    nxt_live = jnp.concatenate([live[1:], jnp.zeros((1,), bool)])
    clast = (jnp.concatenate([cj[1:] != cj[:-1], jnp.ones((1,), bool)]) | ~nxt_live) & live
    comb = (ci, cj, cfirst.astype(I32) + 2 * clast.astype(I32))

    n_valid = (gend[-1] // ts).astype(I32).reshape(1)
    tstart = jnp.arange(ni, dtype=I32) * ts
    te = jnp.sum(tstart[:, None] >= gend[None, :], axis=1).astype(I32)
    te_last = jnp.sum((n_valid[0] - 1) * ts >= gend).astype(I32)
    te = jnp.where(jnp.arange(ni) < n_valid[0], jnp.minimum(te, n_exp - 1), te_last)
    return pos, posc, disp, comb, n_items, te, n_valid, ni * ts


def _moe(x1, xn, ri, rg, cnt, wg, wu, wd, final_g, tt):
    n_exp = wg.shape[0]
    ts = tt
    pos, posc, disp, comb, n_items, te, n_valid, rows = _route_tables(ri, cnt, n_exp, tt, ts)
    xs, gsort = _dispatch(xn, pos, rg, disp, n_items, ts, tt, rows)
    ys = _experts(xs, gsort, wg, wu, wd, te, n_valid, ts)
    return _combine(x1, ys, posc, comb, n_items, final_g, ts, tt)


def kernel(x_prompt, x_sample, cache_k, cache_v, state_conv, page_table, norm_mix, w_in, sb_bias, w_a,
           conv_w, w_b, w_o, norm_ffn, ffn_wg, ffn_wu, ffn_wd, router, moe_wg, moe_wu, moe_wd, norm_final):
    b, s, d = x_prompt.shape
    db, t, _ = x_sample.shape
    depth = w_in.shape[0]
    sbw = w_a.shape[1]
    dc = w_b.shape[1]
    h = sb_bias.shape[1]
    hd = sbw // h
    scale = hd ** -0.5
    ckt = jnp.transpose(cache_k, (0, 1, 3, 4, 2))
    cvt = jnp.transpose(cache_v, (0, 1, 3, 4, 2))
    row2 = lambda g: g.reshape(1, d)

    xp = x_prompt.reshape(b * s, d)
    xs = x_sample.reshape(db * t, d)
    tm_p = _pick(s, (512, 256, 128))
    tm_s = db * t
    kp_l, vp_l, cp_l, ks_l, vs_l, cs_l = [], [], [], [], [], []
    trow = jnp.arange(db * t) % t
    for l in range(depth):
        win = w_in[l].astype(BF16)
        wkv_t = w_in[l][:, sbw:3 * sbw].T.astype(BF16)
        wa, wb, wo = w_a[l].astype(BF16), w_b[l].astype(BF16), w_o[l].astype(BF16)
        last = l == depth - 1
        final_g = row2(norm_final) if last else None
        fi = l // 2
        if l % 2 == 0:
            mode = "dense"
            ffn = (ffn_wg[fi].astype(BF16), ffn_wu[fi].astype(BF16), ffn_wd[fi].astype(BF16))
        else:
            mode = "moe"
            ffn = router[fi]
            ewg, ewu, ewd = moe_wg[fi].astype(BF16), moe_wu[fi].astype(BF16), moe_wd[fi].astype(BF16)

        q, kt, vt, ktb, vtb, bg, u, sga, sgb = _in_proj(xp, row2(norm_mix[l]), win, wkv_t, b, s, sbw, dc, scale)
        o = _sb_prompt(q, ktb, vtb, sb_bias[l], hd)
        res = _mix_out(xp, o, bg, u, sga, sgb, conv_w[l], wa, wb, wo, row2(norm_ffn[l]), seq=s, hist=None,
                       mode=mode, ffn=ffn, final_g=final_g if mode == "dense" else None, tm=tm_p)
        xp = res if mode == "dense" else _moe(*res, ewg, ewu, ewd, final_g, tm_p)
        kp_l.append(kt)
        vp_l.append(vt)
        cp_l.append(u.reshape(b, s, dc)[:, s - (conv_w.shape[1] - 1):])

        q, kt, vt, _, _, bg, u, sga, sgb = _in_proj(xs, row2(norm_mix[l]), win, wkv_t, 1, db * t, sbw, dc, scale)
        o = _sb_sample(q.reshape(db, t, sbw), kt[0], vt[0], sb_bias[l], ckt, cvt, l, page_table, hd)
        st = state_conv[l]
        h1 = jnp.repeat(st[:, 1], t, axis=0)
        h2 = jnp.where((trow == 0)[:, None], jnp.repeat(st[:, 0], t, axis=0), h1)
        res = _mix_out(xs, o.reshape(db * t, sbw), bg, u, sga, sgb, conv_w[l], wa, wb, wo, row2(norm_ffn[l]),
                       seq=t, hist=(h1, h2), mode=mode, ffn=ffn,
                       final_g=final_g if mode == "dense" else None, tm=tm_s)
        xs = res if mode == "dense" else _moe(*res, ewg, ewu, ewd, final_g, tm_s)
        ks_l.append(kt[0].T.reshape(db, t, h, hd))
        vs_l.append(vt[0].T.reshape(db, t, h, hd))
        cs_l.append(jnp.concatenate([st, u.reshape(db, t, dc)], axis=1)[:, t:])

    seq_major = lambda xs_: jnp.transpose(jnp.stack(xs_).reshape(depth, b, h, hd, s), (0, 1, 4, 2, 3))
    return (xp.reshape(b, s, d), xs.reshape(db, t, d), seq_major(kp_l), seq_major(vp_l), jnp.stack(cp_l),
            jnp.stack(ks_l), jnp.stack(vs_l), jnp.stack(cs_l))
```

```python
import functools

import jax
import jax.numpy as jnp
from jax import lax
from jax.experimental import pallas as pl
from jax.experimental.pallas import tpu as pltpu

F32 = jnp.float32
BF16 = jnp.bfloat16
I32 = jnp.int32

RMS_EPS = 1e-6
TOP_K = 2
LANES = 128
SUBLANES = 8
MXU_DIM = 256
LOG2E = 1.4426950408889634
VMEM_LIMIT = 56 * 1024 * 1024


def _pick(n, cands):
    for c in cands:
        if n % c == 0:
            return c
    return n


def _params(sem):
    return pltpu.CompilerParams(dimension_semantics=sem, vmem_limit_bytes=VMEM_LIMIT)


def _const_spec(shape):
    nd = len(shape)
    return pl.BlockSpec(shape, lambda *_: (0,) * nd, pipeline_mode=pl.Buffered(1))


def _rms(x, g):
    return x * lax.rsqrt(jnp.mean(x * x, axis=-1, keepdims=True) + RMS_EPS) * g


def _sigmoid(x):
    return 1.0 / (1.0 + jnp.exp(-x))


def _in_proj_kernel(x_ref, g_ref, w_ref, wkv_ref, q_ref, kt_ref, vt_ref, ktb_ref, vtb_ref, bg_ref, u_ref,
                    sga_ref, sgb_ref, *, sbw, dc, d, scale):
    xn = _rms(x_ref[...], g_ref[...]).astype(BF16)

    def proj(lo, n):
        return jnp.dot(xn, w_ref[:, lo:lo + n], preferred_element_type=F32)

    def proj_t(lo):
        return lax.dot_general(wkv_ref[lo:lo + sbw, :], xn, (((1,), (1,)), ((), ())),
                               preferred_element_type=F32)

    q_ref[...] = (proj(0, sbw) * scale).astype(BF16)
    kt = proj_t(0)
    kt_ref[...] = kt
    ktb_ref[...] = kt.astype(BF16)
    vt = proj_t(sbw)
    vt_ref[...] = vt
    vtb_ref[...] = vt.astype(BF16)
    o = 3 * sbw
    bg_ref[...] = proj(o, dc).astype(BF16)
    u_ref[...] = proj(o + dc, dc) * proj(o + 2 * dc, dc)
    o += 3 * dc
    sga_ref[...] = _sigmoid(proj(o, d)).astype(BF16)
    sgb_ref[...] = _sigmoid(proj(o + d, d)).astype(BF16)


def _in_proj(x, g, w, wkv_t, b, s, sbw, dc, scale):
    n, d = x.shape
    tm = _pick(s, (512, 256, 128))
    nt = s // tm
    row = lambda c: pl.BlockSpec((tm, c), lambda i: (i, 0))
    col = lambda: pl.BlockSpec((None, sbw, tm), lambda i: (i // nt, 0, i % nt))
    outs = [(row(sbw), (n, sbw), BF16), (col(), (b, sbw, s), F32), (col(), (b, sbw, s), F32),
            (col(), (b, sbw, s), BF16), (col(), (b, sbw, s), BF16), (row(dc), (n, dc), BF16),
            (row(dc), (n, dc), F32), (row(d), (n, d), BF16), (row(d), (n, d), BF16)]
    return pl.pallas_call(
        functools.partial(_in_proj_kernel, sbw=sbw, dc=dc, d=d, scale=scale),
        grid=(n // tm,),
        in_specs=[row(d), _const_spec((1, d)), _const_spec(w.shape), _const_spec(wkv_t.shape)],
        out_specs=[o[0] for o in outs],
        out_shape=[jax.ShapeDtypeStruct(o[1], o[2]) for o in outs],
        compiler_params=_params(("parallel",)),
        name="in_proj",
    )(x, g, w, wkv_t)


def _sb_scores(q, kt, bias):
    z = jnp.dot(q, kt, preferred_element_type=F32) + bias
    sp = jnp.maximum(z, 0.0) + jnp.log(1.0 + jnp.exp2(-jnp.abs(z))) * LOG2E
    return sp, z - sp


def _sb_suffix(sp, nt, mask):
    if mask is not None:
        sp = jnp.where(mask, sp, 0.0)
    e = jnp.dot(sp.astype(BF16), nt, preferred_element_type=F32)
    return e, e[:, :1] - sp[:, :1]


def _sb_weights(ls, e, c, mask):
    aw = jnp.exp2(ls + e + c)
    if mask is not None:
        aw = jnp.where(mask, aw, 0.0)
    return aw.astype(BF16)


def _sb_pv(aw, vt):
    return lax.dot_general(aw, vt, (((1,), (1,)), ((), ())), preferred_element_type=F32)


def _sb_chunk(q, kt, vt, nt, bias, c, mask):
    sub = nt.shape[0]
    sp, ls = _sb_scores(q, kt, bias)
    aws = []
    for blk in reversed(range(kt.shape[1] // sub)):
        sl = slice(blk * sub, (blk + 1) * sub)
        m = None if mask is None else mask[:, sl]
        e, tot = _sb_suffix(sp[:, sl], nt, m)
        aws.append(_sb_weights(ls[:, sl], e, c, m))
        c = c + tot
    aw = jnp.concatenate(aws[::-1], axis=1) if len(aws) > 1 else aws[0]
    return c, _sb_pv(aw, vt)


def _sb_prompt_kernel(bias_ref, q_ref, kt_ref, vt_ref, nt_ref, o_ref, acc_ref, *, tq, hd, hpg):
    g = pl.program_id(1)
    i = pl.program_id(2)
    w = hpg * hd
    q2 = q_ref[...].astype(F32)
    nt = nt_ref[...]
    lane = lax.broadcasted_iota(I32, (tq, w), 1)
    row = lax.broadcasted_iota(I32, (tq, tq), 0)
    col = lax.broadcasted_iota(I32, (tq, tq), 1)
    causal = col < row
    in_head = [(lane >= a * hd) & (lane < (a + 1) * hd) for a in range(hpg)]
    qs = [jnp.where(in_head[a], q2, 0.0).astype(BF16) for a in range(hpg)]
    bias = [bias_ref[g * hpg + a] for a in range(hpg)]

    def chunk(j, cs, mask):
        kt = kt_ref[:, pl.ds(pl.multiple_of(j * tq, tq), tq)]
        vt = vt_ref[:, pl.ds(pl.multiple_of(j * tq, tq), tq)]
        out = []
        for a in range(hpg):
            c, pv = _sb_chunk(qs[a], kt, vt, nt, bias[a], cs[a], mask)
            if mask is not None:
                acc_ref[a] = pv
            else:
                acc_ref[a] += pv
            out.append(c)
        return tuple(out)

    cs = chunk(i, tuple(jnp.zeros((tq, 1), F32) for _ in range(hpg)), causal)
    lax.fori_loop(0, i, lambda jj, cs: chunk(i - 1 - jj, cs, None), cs)
    out = acc_ref[0]
    for a in range(1, hpg):
        out = jnp.where(in_head[a], acc_ref[a], out)
    o_ref[...] = out.astype(BF16)


def _neg_tri(t):
    j = lax.broadcasted_iota(I32, (t, t), 0)
    s = lax.broadcasted_iota(I32, (t, t), 1)
    return jnp.where(j > s, -1.0, 0.0).astype(BF16)


def _sb_prompt(q, ktb, vtb, bias, hd):
    b, sbw, s = ktb.shape
    n = q.shape[0]
    hpg = max(1, LANES // hd)
    w = hpg * hd
    tq = _pick(s, (512, 256, 128))
    sub = _pick(tq, (MXU_DIM, LANES))
    nq = s // tq
    return pl.pallas_call(
        functools.partial(_sb_prompt_kernel, tq=tq, hd=hd, hpg=hpg),
        grid=(b, sbw // w, nq),
        in_specs=[pl.BlockSpec(memory_space=pltpu.SMEM),
                  pl.BlockSpec((tq, w), lambda bi, g, i: (bi * nq + i, g)),
                  pl.BlockSpec((None, w, s), lambda bi, g, i: (bi, g, 0)),
                  pl.BlockSpec((None, w, s), lambda bi, g, i: (bi, g, 0)),
                  _const_spec((sub, sub))],
        out_specs=pl.BlockSpec((tq, w), lambda bi, g, i: (bi * nq + i, g)),
        out_shape=jax.ShapeDtypeStruct((n, sbw), BF16),
        scratch_shapes=[pltpu.VMEM((hpg, tq, w), F32)],
        compiler_params=_params(("parallel", "parallel", "arbitrary")),
        name="sb_prompt",
    )(bias * LOG2E, q, ktb, vtb, _neg_tri(sub))


def _sb_sample_kernel(pt_ref, q_ref, bias_ref, kn_ref, vn_ref, ntn_ref, nt_ref, *rest, pg, t, h, hd):
    k_refs = rest[:pg]
    v_refs = rest[pg:2 * pg]
    o_ref = rest[2 * pg]
    c_ref, acc_ref = rest[2 * pg + 1:]
    bi = pl.program_id(0)
    s = pl.program_id(1)
    sbw = h * hd
    rows = t * h
    bias = bias_ref[...]
    q = q_ref[...].astype(F32)
    qrows = jnp.concatenate([jnp.broadcast_to(q[i:i + 1], (h, sbw)) for i in range(t)], axis=0)
    r = lax.broadcasted_iota(I32, (rows, sbw), 0)
    cidx = lax.broadcasted_iota(I32, (rows, sbw), 1)
    headmask = (cidx // hd) == (r % h)
    qbd = jnp.where(headmask, qrows, 0.0).astype(BF16)

    @pl.when(s == 0)
    def _():
        ns = kn_ref.shape[1]
        rq = lax.broadcasted_iota(I32, (rows, ns), 0) // h
        ck = lax.broadcasted_iota(I32, (rows, ns), 1)
        mask = (ck // t == bi) & (ck % t < rq)
        c, pv = _sb_chunk(qbd, kn_ref[...].astype(BF16), vn_ref[...].astype(BF16), ntn_ref[...], bias,
                          jnp.zeros((rows, 1), F32), mask)
        c_ref[...] = c
        acc_ref[...] = pv

    nt = nt_ref[...]
    page = nt.shape[0]
    scores = [_sb_scores(qbd, k_refs[i][...].reshape(sbw, page).astype(BF16), bias) for i in range(pg)]
    sums = [_sb_suffix(sp, nt, None) for sp, _ in scores]
    c = c_ref[...]
    pv = acc_ref[...]
    for i in reversed(range(pg)):
        aw = _sb_weights(scores[i][1], sums[i][0], c, None)
        pv = pv + _sb_pv(aw, v_refs[i][...].reshape(sbw, page).astype(BF16))
        c = c + sums[i][1]
    c_ref[...] = c
    acc_ref[...] = pv

    @pl.when(s == pl.num_programs(1) - 1)
    def _():
        acc = jnp.where(headmask, acc_ref[...], 0.0)
        o_ref[...] = jnp.sum(acc.reshape(t, h, sbw), axis=1).astype(BF16)


def _sb_sample(q, kt_new, vt_new, bias, cache_kt, cache_vt, layer, page_table, hd):
    db, t, sbw = q.shape
    h = sbw // hd
    page = cache_kt.shape[4]
    ns = kt_new.shape[1]
    n_pages = page_table.shape[1]
    pg = _pick(n_pages, (8, 4, 2, 1))
    n_steps = n_pages // pg
    rows = t * h
    bias_rows = (jnp.tile(bias, t) * LOG2E).reshape(rows, 1)

    def page_spec(i):
        def imap(bi, s, pt):
            return (layer, pt[bi * n_pages + (n_steps - 1 - s) * pg + i], 0, 0, 0)
        return pl.BlockSpec((None, None, h, hd, page), imap)

    const = lambda shape: pl.BlockSpec(shape, lambda bi, s, pt: (0,) * len(shape))
    qspec = pl.BlockSpec((None, t, sbw), lambda bi, s, pt: (bi, 0, 0))
    gs = pltpu.PrefetchScalarGridSpec(
        num_scalar_prefetch=1,
        grid=(db, n_steps),
        in_specs=[qspec, const((rows, 1)), const((sbw, ns)), const((sbw, ns)), const((ns, ns)),
                  const((page, page))] + [page_spec(i) for i in range(pg)] * 2,
        out_specs=qspec,
        scratch_shapes=[pltpu.VMEM((rows, 1), F32), pltpu.VMEM((rows, sbw), F32)],
    )
    return pl.pallas_call(
        functools.partial(_sb_sample_kernel, pg=pg, t=t, h=h, hd=hd),
        grid_spec=gs,
        out_shape=jax.ShapeDtypeStruct((db, t, sbw), BF16),
        compiler_params=_params(("parallel", "arbitrary")),
        name="sb_sample",
    )(page_table.reshape(-1), q, bias_rows, kt_new, vt_new, _neg_tri(ns), _neg_tri(page),
      *([cache_kt] * pg), *([cache_vt] * pg))


def _mix_out_kernel(*refs, seq, period, mode, final_norm, n_exp):
    it = iter(refs)
    x_ref, o_ref, bg_ref, u_ref = next(it), next(it), next(it), next(it)
    if period is None:
        uprev_ref = next(it)
    else:
        h1_ref, h2_ref = next(it), next(it)
    sga_ref, sgb_ref, cw_ref, wa_ref, wb_ref, wo_ref, gf_ref = (next(it) for _ in range(7))
    if mode == "dense":
        wg_ref, wu_ref, wd_ref = next(it), next(it), next(it)
        if final_norm:
            gn_ref = next(it)
        y_ref = next(it)
    else:
        rt_ref, tri_ref = next(it), next(it)
        x1_ref, xn_ref, ri_ref, rg_ref, cnt_ref = (next(it) for _ in range(5))

    tm = x_ref.shape[0]
    u = u_ref[...]
    row = lax.broadcasted_iota(I32, u.shape, 0)
    u1 = pltpu.roll(u, 1, axis=0)
    u2 = pltpu.roll(u, 2, axis=0)
    if period is None:
        first = (pl.program_id(0) % (seq // tm)) == 0
        keep = jnp.where(first, 0.0, 1.0)
        hm1 = uprev_ref[SUBLANES - 1:SUBLANES, :] * keep
        hm2 = uprev_ref[SUBLANES - 2:SUBLANES - 1, :] * keep
        u1 = jnp.where(row == 0, hm1, u1)
        u2 = jnp.where(row == 0, hm2, jnp.where(row == 1, hm1, u2))
    else:
        u1 = jnp.where(row % period == 0, h1_ref[...], u1)
        u2 = jnp.where(row % period < 2, h2_ref[...], u2)
    cv = cw_ref[0:1, :] * u2 + cw_ref[1:2, :] * u1 + cw_ref[2:3, :] * u

    ya = jnp.dot(o_ref[...], wa_ref[...], preferred_element_type=F32)
    yb = jnp.dot((bg_ref[...].astype(F32) * cv).astype(BF16), wb_ref[...], preferred_element_type=F32)
    mix = (sga_ref[...].astype(F32) * ya + sgb_ref[...].astype(F32) * yb).astype(BF16)
    x1 = x_ref[...] + jnp.dot(mix, wo_ref[...], preferred_element_type=F32)
    xnf = _rms(x1, gf_ref[...])
    xn = xnf.astype(BF16)

    if mode == "dense":
        hg = jnp.dot(xn, wg_ref[...], preferred_element_type=F32)
        hu = jnp.dot(xn, wu_ref[...], preferred_element_type=F32)
        hh = (hg * _sigmoid(hg) * hu).astype(BF16)
        y = x1 + jnp.dot(hh, wd_ref[...], preferred_element_type=F32)
        if final_norm:
            y = _rms(y, gn_ref[...])
        y_ref[...] = y
        return

    x1_ref[...] = x1
    xn_ref[...] = xn
    ep = rt_ref.shape[0]
    logits = lax.dot_general(rt_ref[...], xnf, (((1,), (1,)), ((), ())),
                             precision=lax.Precision.HIGHEST, preferred_element_type=F32)
    ie = lax.broadcasted_iota(I32, (ep, tm), 0)
    logits = jnp.where(ie < n_exp, logits, -jnp.inf)
    m1 = jnp.max(logits, axis=0, keepdims=True)
    i1 = jnp.min(jnp.where(logits == m1, ie, ep), axis=0, keepdims=True)
    l2 = jnp.where(ie == i1, -jnp.inf, logits)
    m2 = jnp.max(l2, axis=0, keepdims=True)
    i2 = jnp.min(jnp.where(l2 == m2, ie, ep), axis=0, keepdims=True)
    e2 = jnp.exp(m2 - m1)
    g1 = 1.0 / (1.0 + e2)
    g2 = e2 / (1.0 + e2)
    oh = jnp.where((ie == i1) | (ie == i2), 1.0, 0.0)
    before = jnp.dot(oh.astype(BF16), tri_ref[...], preferred_element_type=F32)
    w1 = jnp.sum(jnp.where(ie == i1, before, 0.0), axis=0, keepdims=True).astype(I32)
    w2 = jnp.sum(jnp.where(ie == i2, before, 0.0), axis=0, keepdims=True).astype(I32)
    r8 = lax.broadcasted_iota(I32, (SUBLANES, tm), 0)
    ri_ref[...] = jnp.where(r8 == 0, i1, jnp.where(r8 == 1, i2, jnp.where(r8 == 2, w1, jnp.where(r8 == 3, w2, 0))))
    rg_ref[...] = jnp.where(r8 == 0, g1, jnp.where(r8 == 1, g2, 0.0))
    cnt_ref[...] = jnp.broadcast_to(jnp.sum(oh, axis=1, keepdims=True), (ep, LANES))


def _mix_out(x, o, bg, u, sga, sgb, cw, wa, wb, wo, gf, *, seq, hist, mode, ffn, final_g, tm):
    n, d = x.shape
    sbw, dc = o.shape[1], bg.shape[1]
    nt = n // tm
    row = lambda c: pl.BlockSpec((tm, c), lambda i: (i, 0))
    ins = [x, o, bg, u]
    specs = [row(d), row(sbw), row(dc), row(dc)]
    if hist is None:
        assert seq % tm == 0 and tm % SUBLANES == 0
        period = None
        ins.append(u)
        specs.append(pl.BlockSpec((SUBLANES, dc), lambda i: (jnp.maximum(i * (tm // SUBLANES) - 1, 0), 0)))
    else:
        assert tm % seq == 0
        period = seq
        ins += list(hist)
        specs += [row(dc), row(dc)]
    cwp = jnp.pad(cw, ((0, SUBLANES - cw.shape[0]), (0, 0)))
    ins += [sga, sgb, cwp, wa, wb, wo, gf]
    specs += [row(d), row(d)] + [_const_spec(a.shape) for a in (cwp, wa, wb, wo, gf)]
    if mode == "dense":
        ins += list(ffn)
        specs += [_const_spec(a.shape) for a in ffn]
        if final_g is not None:
            ins.append(final_g)
            specs.append(_const_spec(final_g.shape))
        out_specs = row(d)
        out_shape = jax.ShapeDtypeStruct((n, d), F32)
        n_exp = 0
    else:
        router = ffn
        n_exp = router.shape[1]
        ep = -(-n_exp // SUBLANES) * SUBLANES
        rt = jnp.pad(router.T, ((0, ep - n_exp), (0, 0)))
        ti = lax.broadcasted_iota(I32, (tm, tm), 0)
        tj = lax.broadcasted_iota(I32, (tm, tm), 1)
        tri = jnp.where(ti < tj, 1.0, 0.0).astype(BF16)
        ins += [rt, tri]
        specs += [_const_spec(rt.shape), _const_spec(tri.shape)]
        lane_rows = lambda: pl.BlockSpec((SUBLANES, tm), lambda i: (0, i))
        out_specs = [row(d), row(d), lane_rows(), lane_rows(), pl.BlockSpec((None, ep, LANES), lambda i: (i, 0, 0))]
        out_shape = [jax.ShapeDtypeStruct((n, d), F32), jax.ShapeDtypeStruct((n, d), BF16),
                     jax.ShapeDtypeStruct((SUBLANES, n), I32), jax.ShapeDtypeStruct((SUBLANES, n), F32),
                     jax.ShapeDtypeStruct((nt, ep, LANES), F32)]
    return pl.pallas_call(
        functools.partial(_mix_out_kernel, seq=seq, period=period, mode=mode,
                          final_norm=final_g is not None, n_exp=n_exp),
        grid=(nt,),
        in_specs=specs,
        out_specs=out_specs,
        out_shape=out_shape,
        compiler_params=_params(("parallel",)),
        name="mix_out_" + mode,
    )(*ins)


def _dispatch_kernel(ii_ref, ij_ref, fl_ref, n_ref, x_ref, pos_ref, rg_ref, xs_ref, gs_ref, *, ts):
    w = pl.program_id(0)

    @pl.when(w < n_ref[0])
    def _():
        tt = x_ref.shape[0]
        r = ii_ref[w] * ts + lax.broadcasted_iota(I32, (ts, tt), 0)
        m1 = r == pos_ref[0:1, :]
        m2 = r == pos_ref[1:2, :]
        sel = jnp.where(m1 | m2, 1.0, 0.0).astype(BF16)
        xs = jnp.dot(sel, x_ref[...], preferred_element_type=F32).astype(BF16)
        gate = jnp.sum(jnp.where(m1, rg_ref[0:1, :], 0.0) + jnp.where(m2, rg_ref[1:2, :], 0.0),
                       axis=1, keepdims=True)
        gate = jnp.broadcast_to(gate, gs_ref.shape)

        @pl.when(fl_ref[w] == 1)
        def _():
            xs_ref[...] = xs
            gs_ref[...] = gate

        @pl.when(fl_ref[w] == 0)
        def _():
            xs_ref[...] += xs
            gs_ref[...] += gate


def _dispatch(xn, pos, rg, items, n_items, ts, tt, rows):
    n, d = xn.shape
    ii, ij, first = items
    gs = pltpu.PrefetchScalarGridSpec(
        num_scalar_prefetch=4,
        grid=(ii.shape[0],),
        in_specs=[pl.BlockSpec((tt, d), lambda w, ii, ij, fl, nn: (ij[w], 0)),
                  pl.BlockSpec((SUBLANES, tt), lambda w, ii, ij, fl, nn: (0, ij[w])),
                  pl.BlockSpec((SUBLANES, tt), lambda w, ii, ij, fl, nn: (0, ij[w]))],
        out_specs=[pl.BlockSpec((ts, d), lambda w, ii, ij, fl, nn: (ii[w], 0)),
                   pl.BlockSpec((ts, LANES), lambda w, ii, ij, fl, nn: (ii[w], 0))],
    )
    return pl.pallas_call(
        functools.partial(_dispatch_kernel, ts=ts),
        grid_spec=gs,
        out_shape=[jax.ShapeDtypeStruct((rows, d), BF16), jax.ShapeDtypeStruct((rows, LANES), F32)],
        compiler_params=_params(("arbitrary",)),
        name="moe_dispatch",
    )(ii, ij, first, n_items, xn, pos, rg)


def _experts_kernel(te_ref, nv_ref, xs_ref, wg_ref, wu_ref, wd_ref, gs_ref, y_ref, acc_ref):
    i = pl.program_id(0)
    c = pl.program_id(1)

    @pl.when(i < nv_ref[0])
    def _():
        x = xs_ref[...]
        hg = jnp.dot(x, wg_ref[...], preferred_element_type=F32)
        hu = jnp.dot(x, wu_ref[...], preferred_element_type=F32)
        hh = (hg * _sigmoid(hg) * hu).astype(BF16)
        part = jnp.dot(hh, wd_ref[...], preferred_element_type=F32)

        @pl.when(c == 0)
        def _():
            acc_ref[...] = part

        @pl.when(c > 0)
        def _():
            acc_ref[...] += part

        @pl.when(c == pl.num_programs(1) - 1)
        def _():
            y_ref[...] = (acc_ref[...] * gs_ref[:, 0:1]).astype(BF16)


def _experts(xs, gsort, wg, wu, wd, tile_expert, n_valid, ts):
    rows, d = xs.shape
    dff = wg.shape[2]
    fc = _pick(dff, (2048, 1792, 1536, 1280, 1024, 896, 768, 640, 512, 384, 256, 128))
    nfc = dff // fc
    tile = lambda i, nv: jnp.minimum(i, nv[0] - 1)
    chunk = lambda i, c, nv: jnp.where(i < nv[0], c, nfc - 1)
    gs = pltpu.PrefetchScalarGridSpec(
        num_scalar_prefetch=2,
        grid=(rows // ts, nfc),
        in_specs=[pl.BlockSpec((ts, d), lambda i, c, te, nv: (tile(i, nv), 0)),
                  pl.BlockSpec((None, d, fc), lambda i, c, te, nv: (te[i], 0, chunk(i, c, nv))),
                  pl.BlockSpec((None, d, fc), lambda i, c, te, nv: (te[i], 0, chunk(i, c, nv))),
                  pl.BlockSpec((None, fc, d), lambda i, c, te, nv: (te[i], chunk(i, c, nv), 0)),
                  pl.BlockSpec((ts, LANES), lambda i, c, te, nv: (tile(i, nv), 0))],
        out_specs=pl.BlockSpec((ts, d), lambda i, c, te, nv: (tile(i, nv), 0)),
        scratch_shapes=[pltpu.VMEM((ts, d), F32)],
    )
    return pl.pallas_call(
        _experts_kernel,
        grid_spec=gs,
        out_shape=jax.ShapeDtypeStruct((rows, d), BF16),
        compiler_params=_params(("arbitrary", "arbitrary")),
        name="moe_experts",
    )(tile_expert, n_valid, xs, wg, wu, wd, gsort)


def _combine_kernel(ii_ref, ij_ref, fl_ref, n_ref, x1_ref, ys_ref, posc_ref, gn_ref, y_ref, *, ts, final_norm):
    w = pl.program_id(0)

    @pl.when(w < n_ref[0])
    def _():
        tt = x1_ref.shape[0]
        r = ii_ref[w] * ts + lax.broadcasted_iota(I32, (tt, ts), 1)
        sel = jnp.where((r == posc_ref[:, 0:1]) | (r == posc_ref[:, 1:2]), 1.0, 0.0).astype(BF16)
        part = jnp.dot(sel, ys_ref[...], preferred_element_type=F32)
        fl = fl_ref[w]

        @pl.when(fl % 2 == 1)
        def _():
            y_ref[...] = x1_ref[...] + part

        @pl.when(fl % 2 == 0)
        def _():
            y_ref[...] += part

        if final_norm:
            @pl.when(fl >= 2)
            def _():
                y_ref[...] = _rms(y_ref[...], gn_ref[...])


def _combine(x1, ys, posc, items, n_items, final_g, ts, tt):
    n, d = x1.shape
    ii, ij, flags = items
    gn = final_g if final_g is not None else jnp.ones((1, d), F32)
    gs = pltpu.PrefetchScalarGridSpec(
        num_scalar_prefetch=4,
        grid=(ii.shape[0],),
        in_specs=[pl.BlockSpec((tt, d), lambda w, ii, ij, fl, nn: (ij[w], 0)),
                  pl.BlockSpec((ts, d), lambda w, ii, ij, fl, nn: (ii[w], 0)),
                  pl.BlockSpec((tt, LANES), lambda w, ii, ij, fl, nn: (ij[w], 0)),
                  pl.BlockSpec((1, d), lambda w, ii, ij, fl, nn: (0, 0))],
        out_specs=pl.BlockSpec((tt, d), lambda w, ii, ij, fl, nn: (ij[w], 0)),
    )
    return pl.pallas_call(
        functools.partial(_combine_kernel, ts=ts, final_norm=final_g is not None),
        grid_spec=gs,
        out_shape=jax.ShapeDtypeStruct((n, d), F32),
        compiler_params=_params(("arbitrary",)),
        name="moe_combine",
    )(ii, ij, flags, n_items, x1, ys, posc, gn)


def _route_tables(ri, cnt, n_exp, tt, ts):
    n = ri.shape[1]
    nj = n // tt
    cnt = cnt[:, :n_exp, 0].astype(I32)
    tot = jnp.sum(cnt, axis=0)
    gsize = (tot + ts - 1) // ts * ts
    gend = jnp.cumsum(gsize)
    goff = gend - gsize
    seg_start = goff[None, :] + jnp.cumsum(cnt, axis=0) - cnt
    e1, e2, w1, w2 = ri[0], ri[1], ri[2], ri[3]
    oh = lambda e: e[:, None] == jnp.arange(n_exp, dtype=I32)[None, :]
    base = jnp.repeat(seg_start, tt, axis=0)
    pos1 = jnp.sum(jnp.where(oh(e1), base, 0), axis=1) + w1
    pos2 = jnp.sum(jnp.where(oh(e2), base, 0), axis=1) + w2
    pos = jnp.zeros((SUBLANES, n), I32).at[0].set(pos1).at[1].set(pos2)
    posc = jnp.zeros((n, LANES), I32).at[:, 0].set(pos1).at[:, 1].set(pos2)

    valid = cnt > 0
    i0 = seg_start // ts
    i1 = (seg_start + cnt - 1) // ts
    jj = jnp.broadcast_to(jnp.arange(nj, dtype=I32)[:, None], cnt.shape)
    it_i = jnp.concatenate([i0.reshape(-1), i1.reshape(-1)])
    it_j = jnp.concatenate([jj.reshape(-1), jj.reshape(-1)])
    it_v = jnp.concatenate([valid.reshape(-1), (valid & (i1 > i0)).reshape(-1)])
    n_items = jnp.sum(it_v).astype(I32).reshape(1)
    big = jnp.iinfo(jnp.int32).max
    ni = n * TOP_K // ts + n_exp

    def ordered(key):
        order = jnp.argsort(jnp.where(it_v, key, big))
        last = jnp.maximum(n_items[0] - 1, 0)
        idx = jnp.minimum(jnp.arange(order.shape[0]), last)
        si, sj = it_i[order][idx], it_j[order][idx]
        live = jnp.arange(order.shape[0]) < n_items[0]
        return si, sj, live

    si, sj, live = ordered(it_i * nj + it_j)
    first = jnp.concatenate([jnp.ones((1,), bool), si[1:] != si[:-1]]) & live
    disp = (si, sj, first.astype(I32))
    ci, cj, live = ordered(it_j * ni + it_i)
    cfirst = jnp.concatenate([jnp.ones((1,), bool), cj[1:] != cj[:-1]]) & live
    nxt_live = jnp.concatenate([live[1:], jnp.zeros((1,), bool)])
    clast = (jnp.concatenate([cj[1:] != cj[:-1], jnp.ones((1,), bool)]) | ~nxt_live) & live
    comb = (ci, cj, cfirst.astype(I32) + 2 * clast.astype(I32))

    n_valid = (gend[-1] // ts).astype(I32).reshape(1)
    tstart = jnp.arange(ni, dtype=I32) * ts
    te = jnp.sum(tstart[:, None] >= gend[None, :], axis=1).astype(I32)
    te_last = jnp.sum((n_valid[0] - 1) * ts >= gend).astype(I32)
    te = jnp.where(jnp.arange(ni) < n_valid[0], jnp.minimum(te, n_exp - 1), te_last)
    return pos, posc, disp, comb, n_items, te, n_valid, ni * ts


def _moe(x1, xn, ri, rg, cnt, wg, wu, wd, final_g, tt):
    n_exp = wg.shape[0]
    ts = tt
    pos, posc, disp, comb, n_items, te, n_valid, rows = _route_tables(ri, cnt, n_exp, tt, ts)
    xs, gsort = _dispatch(xn, pos, rg, disp, n_items, ts, tt, rows)
    ys = _experts(xs, gsort, wg, wu, wd, te, n_valid, ts)
    return _combine(x1, ys, posc, comb, n_items, final_g, ts, tt)


def kernel(x_prompt, x_sample, cache_k, cache_v, state_conv, page_table, norm_mix, w_in, sb_bias, w_a,
           conv_w, w_b, w_o, norm_ffn, ffn_wg, ffn_wu, ffn_wd, router, moe_wg, moe_wu, moe_wd, norm_final):
    b, s, d = x_prompt.shape
    db, t, _ = x_sample.shape
    depth = w_in.shape[0]
    sbw = w_a.shape[1]
    dc = w_b.shape[1]
    h = sb_bias.shape[1]
    hd = sbw // h
    scale = hd ** -0.5 * LOG2E
    ckt = jnp.transpose(cache_k, (0, 1, 3, 4, 2))
    cvt = jnp.transpose(cache_v, (0, 1, 3, 4, 2))
    row2 = lambda g: g.reshape(1, d)

    xp = x_prompt.reshape(b * s, d)
    xs = x_sample.reshape(db * t, d)
    tm_p = _pick(s, (512, 256, 128))
    tm_s = db * t
    kp_l, vp_l, cp_l, ks_l, vs_l, cs_l = [], [], [], [], [], []
    trow = jnp.arange(db * t) % t
    for l in range(depth):
        win = w_in[l].astype(BF16)
        wkv_t = w_in[l][:, sbw:3 * sbw].T.astype(BF16)
        wa, wb, wo = w_a[l].astype(BF16), w_b[l].astype(BF16), w_o[l].astype(BF16)
        last = l == depth - 1
        final_g = row2(norm_final) if last else None
        fi = l // 2
        if l % 2 == 0:
            mode = "dense"
            ffn = (ffn_wg[fi].astype(BF16), ffn_wu[fi].astype(BF16), ffn_wd[fi].astype(BF16))
        else:
            mode = "moe"
            ffn = router[fi]
            ewg, ewu, ewd = moe_wg[fi].astype(BF16), moe_wu[fi].astype(BF16), moe_wd[fi].astype(BF16)

        q, kt, vt, ktb, vtb, bg, u, sga, sgb = _in_proj(xp, row2(norm_mix[l]), win, wkv_t, b, s, sbw, dc, scale)
        o = _sb_prompt(q, ktb, vtb, sb_bias[l], hd)
        res = _mix_out(xp, o, bg, u, sga, sgb, conv_w[l], wa, wb, wo, row2(norm_ffn[l]), seq=s, hist=None,
                       mode=mode, ffn=ffn, final_g=final_g if mode == "dense" else None, tm=tm_p)
        xp = res if mode == "dense" else _moe(*res, ewg, ewu, ewd, final_g, tm_p)
        kp_l.append(kt)
        vp_l.append(vt)
        cp_l.append(u.reshape(b, s, dc)[:, s - (conv_w.shape[1] - 1):])

        q, kt, vt, _, _, bg, u, sga, sgb = _in_proj(xs, row2(norm_mix[l]), win, wkv_t, 1, db * t, sbw, dc, scale)
        o = _sb_sample(q.reshape(db, t, sbw), kt[0], vt[0], sb_bias[l], ckt, cvt, l, page_table, hd)
        st = state_conv[l]
        h1 = jnp.repeat(st[:, 1], t, axis=0)
        h2 = jnp.where((trow == 0)[:, None], jnp.repeat(st[:, 0], t, axis=0), h1)
        res = _mix_out(xs, o.reshape(db * t, sbw), bg, u, sga, sgb, conv_w[l], wa, wb, wo, row2(norm_ffn[l]),
                       seq=t, hist=(h1, h2), mode=mode, ffn=ffn,
                       final_g=final_g if mode == "dense" else None, tm=tm_s)
        xs = res if mode == "dense" else _moe(*res, ewg, ewu, ewd, final_g, tm_s)
        ks_l.append(kt[0].T.reshape(db, t, h, hd))
        vs_l.append(vt[0].T.reshape(db, t, h, hd))
        cs_l.append(jnp.concatenate([st, u.reshape(db, t, dc)], axis=1)[:, t:])

    seq_major = lambda xs_: jnp.transpose(jnp.stack(xs_).reshape(depth, b, h, hd, s), (0, 1, 4, 2, 3))
    return (xp.reshape(b, s, d), xs.reshape(db, t, d), seq_major(kp_l), seq_major(vp_l), jnp.stack(cp_l),
            jnp.stack(ks_l), jnp.stack(vs_l), jnp.stack(cs_l))
```

```python
import functools

import jax
import jax.numpy as jnp
from jax import lax
from jax.experimental import pallas as pl
from jax.experimental.pallas import tpu as pltpu

F32 = jnp.float32
BF16 = jnp.bfloat16
I32 = jnp.int32

RMS_EPS = 1e-6
TOP_K = 2
LANES = 128
SUBLANES = 8
MXU_DIM = 256
LOG2E = 1.4426950408889634
BIAS_TERMS = 3
WIN = 128
VMEM_LIMIT = 56 * 1024 * 1024


def _pick(n, cands):
    for c in cands:
        if n % c == 0:
            return c
    return n


def _params(sem):
    return pltpu.CompilerParams(dimension_semantics=sem, vmem_limit_bytes=VMEM_LIMIT)


def _const_spec(shape):
    nd = len(shape)
    return pl.BlockSpec(shape, lambda *_: (0,) * nd, pipeline_mode=pl.Buffered(1))


def _rms(x, g):
    return x * lax.rsqrt(jnp.mean(x * x, axis=-1, keepdims=True) + RMS_EPS) * g


def _sigmoid(x):
    return 1.0 / (1.0 + jnp.exp(-x))


def _in_proj_kernel(x_ref, g_ref, w_ref, wkv_ref, *rest, sbw, dc, d, scale):
    q_ref, kt_ref, vt_ref, ktb_ref, vtb_ref, bg_ref, u_ref, sga_ref, sgb_ref = rest[-9:]
    xn = _rms(x_ref[...], g_ref[...]).astype(BF16)

    def proj(lo, n):
        return jnp.dot(xn, w_ref[:, lo:lo + n], preferred_element_type=F32)

    def proj_t(lo):
        return lax.dot_general(wkv_ref[lo:lo + sbw, :], xn, (((1,), (1,)), ((), ())),
                               preferred_element_type=F32)

    q_ref[...] = (proj(0, sbw) * scale).astype(BF16)
    kt = proj_t(0)
    kt_ref[...] = kt
    ktb_ref[...] = kt.astype(BF16)
    vt = proj_t(sbw)
    vt_ref[...] = vt
    vtb_ref[...] = vt.astype(BF16)
    o = 3 * sbw
    bg_ref[...] = proj(o, dc).astype(BF16)
    u_ref[...] = proj(o + dc, dc) * proj(o + 2 * dc, dc)
    o += 3 * dc
    sga_ref[...] = _sigmoid(proj(o, d)).astype(BF16)
    sgb_ref[...] = _sigmoid(proj(o + d, d)).astype(BF16)


def _in_proj(x, g, w, wkv_t, b, s, sbw, dc, scale, layer=0, depth=1, kv_all=None):
    n, d = x.shape
    tm = _pick(s, (512, 256, 128))
    nt = s // tm
    row = lambda c: pl.BlockSpec((tm, c), lambda i: (i, 0))
    col = lambda: pl.BlockSpec((None, sbw, tm), lambda i: (i // nt, 0, i % nt))
    kv = lambda: pl.BlockSpec((None, None, sbw, tm), lambda i: (layer, i // nt, 0, i % nt))
    outs = [(row(sbw), (n, sbw), BF16), (kv(), (depth, b, sbw, s), F32), (kv(), (depth, b, sbw, s), F32),
            (col(), (b, sbw, s), BF16), (col(), (b, sbw, s), BF16), (row(dc), (n, dc), BF16),
            (row(dc), (n, dc), F32), (row(d), (n, d), BF16), (row(d), (n, d), BF16)]
    ins = [x, g, w, wkv_t]
    specs = [row(d), _const_spec((1, d)), _const_spec(w.shape), _const_spec(wkv_t.shape)]
    aliases = {}
    if kv_all is not None:
        aliases = {len(ins): 1, len(ins) + 1: 2}
        ins += list(kv_all)
        specs += [pl.BlockSpec(memory_space=pl.ANY)] * 2
    return pl.pallas_call(
        functools.partial(_in_proj_kernel, sbw=sbw, dc=dc, d=d, scale=scale),
        grid=(n // tm,),
        in_specs=specs,
        out_specs=[o[0] for o in outs],
        out_shape=[jax.ShapeDtypeStruct(o[1], o[2]) for o in outs],
        input_output_aliases=aliases,
        compiler_params=_params(("parallel",)),
        name="in_proj",
    )(*ins)


def _sb_scores(q, kt, bias):
    z = jnp.dot(q, kt, preferred_element_type=F32)
    if bias is not None:
        z = z + bias
    sp = jnp.maximum(z, 0.0) + jnp.log(1.0 + jnp.exp2(-jnp.abs(z))) * LOG2E
    return sp.astype(BF16), z


def _sb_suffix(sp, nt, mask):
    if mask is not None:
        sp = jnp.where(mask, sp, jnp.zeros_like(sp))
    e = jnp.dot(sp, nt, preferred_element_type=F32)
    return e, e[:, :1]


def _sb_weights(z, e, c, mask):
    aw = jnp.exp2(z + e + c)
    if mask is not None:
        aw = jnp.where(mask, aw, 0.0)
    return aw.astype(BF16)


def _sb_pv(aw, vt):
    return lax.dot_general(aw, vt, (((1,), (1,)), ((), ())), preferred_element_type=F32)


def _sb_chunk(q, kt, vt, nt, bias, c, mask):
    sub = nt.shape[0]
    sp, z = _sb_scores(q, kt, bias)
    aws = []
    for blk in reversed(range(kt.shape[1] // sub)):
        sl = slice(blk * sub, (blk + 1) * sub)
        m = None if mask is None else mask[:, sl]
        e, tot = _sb_suffix(sp[:, sl], nt, m)
        aws.append(_sb_weights(z[:, sl], e, c, m))
        c = c + tot
    aw = jnp.concatenate(aws[::-1], axis=1) if len(aws) > 1 else aws[0]
    return c, _sb_pv(aw, vt)


def _sb_prompt_kernel(bias_ref, q_ref, kt_ref, vt_ref, nt_ref, o_ref, acc_ref, *, tq, hd, hpg, ng):
    g = pl.program_id(1)
    i = pl.program_id(2)
    w = hpg * hd
    nt = nt_ref[...]
    lane = lax.broadcasted_iota(I32, (tq, w), 1)
    row = lax.broadcasted_iota(I32, (tq, tq), 0)
    col = lax.broadcasted_iota(I32, (tq, tq), 1)
    causal = col < row
    in_head = [(lane >= a * hd) & (lane < (a + 1) * hd) for a in range(hpg)]
    qs = []
    for gi in range(ng):
        q2 = q_ref[:, gi * w:(gi + 1) * w].astype(F32)
        for a in range(hpg):
            head = (g * ng + gi) * hpg + a
            ext = jnp.zeros((tq, w), F32)
            for k in range(BIAS_TERMS):
                ext = jnp.where(lane == k, bias_ref[head * BIAS_TERMS + k], ext)
            qs.append(jnp.concatenate([jnp.where(in_head[a], q2, 0.0), ext], axis=1).astype(BF16))
    ones_rows = jnp.where(lax.broadcasted_iota(I32, (w, tq), 0) < BIAS_TERMS, 1.0, 0.0).astype(BF16)

    sub = nt.shape[0]

    def chunk(j, cs, diagonal):
        keys = pl.ds(pl.multiple_of(j * tq, tq), tq)
        out = []
        for gi in range(ng):
            kt = jnp.concatenate([kt_ref[gi * w:(gi + 1) * w, keys], ones_rows], axis=0)
            vt = vt_ref[gi * w:(gi + 1) * w, keys]
            for a in range(hpg):
                k = gi * hpg + a
                if diagonal:
                    parts = []
                    for rb in range(tq // sub):
                        rows = slice(rb * sub, (rb + 1) * sub)
                        nk = (rb + 1) * sub
                        c, pv = _sb_chunk(qs[k][rows], kt[:, :nk], vt[:, :nk], nt, None,
                                          jnp.zeros((sub, 1), F32), causal[rows, :nk])
                        acc_ref[k, rows, :] = pv
                        parts.append(c)
                    out.append(jnp.concatenate(parts, axis=0))
                else:
                    c, pv = _sb_chunk(qs[k], kt, vt, nt, None, cs[k], None)
                    acc_ref[k] += pv
                    out.append(c)
        return tuple(out)

    cs = chunk(i, None, True)
    lax.fori_loop(0, i, lambda jj, cs: chunk(i - 1 - jj, cs, False), cs)
    for gi in range(ng):
        out = acc_ref[gi * hpg]
        for a in range(1, hpg):
            out = jnp.where(in_head[a], acc_ref[gi * hpg + a], out)
        o_ref[:, gi * w:(gi + 1) * w] = out.astype(BF16)


def _split_bf16(x):
    terms = []
    for _ in range(BIAS_TERMS):
        t = x.astype(BF16).astype(F32)
        terms.append(t)
        x = x - t
    return jnp.stack(terms, axis=-1)


def _neg_tri(t):
    j = lax.broadcasted_iota(I32, (t, t), 0)
    s = lax.broadcasted_iota(I32, (t, t), 1)
    return jnp.where(j >= s, -1.0, 0.0).astype(BF16)


def _sb_prompt(q, ktb, vtb, bias, hd):
    b, sbw, s = ktb.shape
    n = q.shape[0]
    hpg = max(1, LANES // hd)
    w = hpg * hd
    ng = _pick(sbw // w, (2, 1))
    gw = ng * w
    tq = _pick(s, (512, 256, 128))
    sub = _pick(tq, (MXU_DIM, LANES))
    nq = s // tq
    return pl.pallas_call(
        functools.partial(_sb_prompt_kernel, tq=tq, hd=hd, hpg=hpg, ng=ng),
        grid=(b, sbw // gw, nq),
        in_specs=[pl.BlockSpec(memory_space=pltpu.SMEM),
                  pl.BlockSpec((tq, gw), lambda bi, g, i: (bi * nq + i, g)),
                  pl.BlockSpec((None, gw, s), lambda bi, g, i: (bi, g, 0)),
                  pl.BlockSpec((None, gw, s), lambda bi, g, i: (bi, g, 0)),
                  _const_spec((sub, sub))],
        out_specs=pl.BlockSpec((tq, gw), lambda bi, g, i: (bi * nq + i, g)),
        out_shape=jax.ShapeDtypeStruct((n, sbw), BF16),
        scratch_shapes=[pltpu.VMEM((ng * hpg, tq, w), F32)],
        compiler_params=_params(("parallel", "parallel", "arbitrary")),
        name="sb_prompt",
    )(_split_bf16(bias * LOG2E).reshape(-1), q, ktb, vtb, _neg_tri(sub))


def _sb_sample_kernel(pt_ref, q_ref, bias_ref, kn_ref, vn_ref, ntn_ref, nt_ref, *rest, pg, t, h, hd):
    k_refs = rest[:pg]
    v_refs = rest[pg:2 * pg]
    o_ref = rest[2 * pg]
    c_ref, acc_ref = rest[2 * pg + 1:]
    bi = pl.program_id(0)
    s = pl.program_id(1)
    sbw = h * hd
    rows = t * h
    bias = bias_ref[...]
    q = q_ref[...].astype(F32)
    qrows = jnp.concatenate([jnp.broadcast_to(q[i:i + 1], (h, sbw)) for i in range(t)], axis=0)
    r = lax.broadcasted_iota(I32, (rows, sbw), 0)
    cidx = lax.broadcasted_iota(I32, (rows, sbw), 1)
    headmask = (cidx // hd) == (r % h)
    qbd = jnp.where(headmask, qrows, 0.0).astype(BF16)

    @pl.when(s == 0)
    def _():
        ns = kn_ref.shape[1]
        rq = lax.broadcasted_iota(I32, (rows, ns), 0) // h
        ck = lax.broadcasted_iota(I32, (rows, ns), 1)
        mask = (ck // t == bi) & (ck % t < rq)
        c, pv = _sb_chunk(qbd, kn_ref[...].astype(BF16), vn_ref[...].astype(BF16), ntn_ref[...], bias,
                          jnp.zeros((rows, 1), F32), mask)
        c_ref[...] = c
        acc_ref[...] = pv

    nt = nt_ref[...]
    page = nt.shape[0]
    scores = [_sb_scores(qbd, k_refs[i][...].reshape(sbw, page).astype(BF16), bias) for i in range(pg)]
    sums = [_sb_suffix(sp, nt, None) for sp, _ in scores]
    c = c_ref[...]
    pv = acc_ref[...]
    for i in reversed(range(pg)):
        aw = _sb_weights(scores[i][1], sums[i][0], c, None)
        pv = pv + _sb_pv(aw, v_refs[i][...].reshape(sbw, page).astype(BF16))
        c = c + sums[i][1]
    c_ref[...] = c
    acc_ref[...] = pv

    @pl.when(s == pl.num_programs(1) - 1)
    def _():
        acc = jnp.where(headmask, acc_ref[...], 0.0)
        o_ref[...] = jnp.sum(acc.reshape(t, h, sbw), axis=1).astype(BF16)


def _sb_sample(q, kt_new, vt_new, bias, cache_kt, cache_vt, layer, page_table, hd):
    db, t, sbw = q.shape
    h = sbw // hd
    page = cache_kt.shape[4]
    ns = kt_new.shape[1]
    n_pages = page_table.shape[1]
    pg = _pick(n_pages, (16, 8, 4, 2, 1))
    n_steps = n_pages // pg
    rows = t * h
    bias_rows = (jnp.tile(bias, t) * LOG2E).reshape(rows, 1)

    def page_spec(i):
        def imap(bi, s, pt):
            return (layer, pt[bi * n_pages + (n_steps - 1 - s) * pg + i], 0, 0, 0)
        return pl.BlockSpec((None, None, h, hd, page), imap)

    const = lambda shape: pl.BlockSpec(shape, lambda bi, s, pt: (0,) * len(shape))
    qspec = pl.BlockSpec((None, t, sbw), lambda bi, s, pt: (bi, 0, 0))
    gs = pltpu.PrefetchScalarGridSpec(
        num_scalar_prefetch=1,
        grid=(db, n_steps),
        in_specs=[qspec, const((rows, 1)), const((sbw, ns)), const((sbw, ns)), const((ns, ns)),
                  const((page, page))] + [page_spec(i) for i in range(pg)] * 2,
        out_specs=qspec,
        scratch_shapes=[pltpu.VMEM((rows, 1), F32), pltpu.VMEM((rows, sbw), F32)],
    )
    return pl.pallas_call(
        functools.partial(_sb_sample_kernel, pg=pg, t=t, h=h, hd=hd),
        grid_spec=gs,
        out_shape=jax.ShapeDtypeStruct((db, t, sbw), BF16),
        compiler_params=_params(("parallel", "arbitrary")),
        name="sb_sample",
    )(page_table.reshape(-1), q, bias_rows, kt_new, vt_new, _neg_tri(ns), _neg_tri(page),
      *([cache_kt] * pg), *([cache_vt] * pg))


def _mix_out_kernel(*refs, seq, period, mode, final_norm, n_exp):
    it = iter(refs)
    x_ref, o_ref, bg_ref, u_ref = next(it), next(it), next(it), next(it)
    if period is None:
        uprev_ref = next(it)
    else:
        h1_ref, h2_ref = next(it), next(it)
    sga_ref, sgb_ref, cw_ref, wa_ref, wb_ref, wo_ref, gf_ref = (next(it) for _ in range(7))
    if mode == "dense":
        wg_ref, wu_ref, wd_ref = next(it), next(it), next(it)
        if final_norm:
            gn_ref = next(it)
        y_ref = next(it)
    else:
        rt_ref, tri_ref = next(it), next(it)
        x1_ref, xn_ref, ri_ref, rg_ref, cnt_ref = (next(it) for _ in range(5))

    tm = x_ref.shape[0]
    u = u_ref[...]
    row = lax.broadcasted_iota(I32, u.shape, 0)
    u1 = pltpu.roll(u, 1, axis=0)
    u2 = pltpu.roll(u, 2, axis=0)
    if period is None:
        first = (pl.program_id(0) % (seq // tm)) == 0
        keep = jnp.where(first, 0.0, 1.0)
        hm1 = uprev_ref[SUBLANES - 1:SUBLANES, :] * keep
        hm2 = uprev_ref[SUBLANES - 2:SUBLANES - 1, :] * keep
        u1 = jnp.where(row == 0, hm1, u1)
        u2 = jnp.where(row == 0, hm2, jnp.where(row == 1, hm1, u2))
    else:
        u1 = jnp.where(row % period == 0, h1_ref[...], u1)
        u2 = jnp.where(row % period < 2, h2_ref[...], u2)
    cv = cw_ref[0:1, :] * u2 + cw_ref[1:2, :] * u1 + cw_ref[2:3, :] * u

    ya = jnp.dot(o_ref[...], wa_ref[...], preferred_element_type=F32)
    yb = jnp.dot((bg_ref[...].astype(F32) * cv).astype(BF16), wb_ref[...], preferred_element_type=F32)
    mix = (sga_ref[...].astype(F32) * ya + sgb_ref[...].astype(F32) * yb).astype(BF16)
    x1 = x_ref[...] + jnp.dot(mix, wo_ref[...], preferred_element_type=F32)
    xnf = _rms(x1, gf_ref[...])
    xn = xnf.astype(BF16)

    if mode == "dense":
        hg = jnp.dot(xn, wg_ref[...], preferred_element_type=F32)
        hu = jnp.dot(xn, wu_ref[...], preferred_element_type=F32)
        hh = (hg * _sigmoid(hg) * hu).astype(BF16)
        y = x1 + jnp.dot(hh, wd_ref[...], preferred_element_type=F32)
        if final_norm:
            y = _rms(y, gn_ref[...])
        y_ref[...] = y
        return

    x1_ref[...] = x1
    xn_ref[...] = xn
    ep = rt_ref.shape[0]
    logits = lax.dot_general(rt_ref[...], xnf, (((1,), (1,)), ((), ())),
                             precision=lax.Precision.HIGHEST, preferred_element_type=F32)
    ie = lax.broadcasted_iota(I32, (ep, tm), 0)
    logits = jnp.where(ie < n_exp, logits, -jnp.inf)
    m1 = jnp.max(logits, axis=0, keepdims=True)
    i1 = jnp.min(jnp.where(logits == m1, ie, ep), axis=0, keepdims=True)
    l2 = jnp.where(ie == i1, -jnp.inf, logits)
    m2 = jnp.max(l2, axis=0, keepdims=True)
    i2 = jnp.min(jnp.where(l2 == m2, ie, ep), axis=0, keepdims=True)
    e2 = jnp.exp(m2 - m1)
    g1 = 1.0 / (1.0 + e2)
    g2 = e2 / (1.0 + e2)
    oh = jnp.where((ie == i1) | (ie == i2), 1.0, 0.0)
    before = jnp.dot(oh.astype(BF16), tri_ref[...], preferred_element_type=F32)
    w1 = jnp.sum(jnp.where(ie == i1, before, 0.0), axis=0, keepdims=True).astype(I32)
    w2 = jnp.sum(jnp.where(ie == i2, before, 0.0), axis=0, keepdims=True).astype(I32)
    r8 = lax.broadcasted_iota(I32, (SUBLANES, tm), 0)
    ri_ref[...] = jnp.where(r8 == 0, i1, jnp.where(r8 == 1, i2, jnp.where(r8 == 2, w1, jnp.where(r8 == 3, w2, 0))))
    rg_ref[...] = jnp.where(r8 == 0, g1, jnp.where(r8 == 1, g2, 0.0))
    cnt_ref[...] = jnp.broadcast_to(jnp.sum(oh, axis=1, keepdims=True), (ep, LANES))


def _mix_out(x, o, bg, u, sga, sgb, cw, wa, wb, wo, gf, *, seq, hist, mode, ffn, final_g, tm):
    n, d = x.shape
    sbw, dc = o.shape[1], bg.shape[1]
    nt = n // tm
    row = lambda c: pl.BlockSpec((tm, c), lambda i: (i, 0))
    ins = [x, o, bg, u]
    specs = [row(d), row(sbw), row(dc), row(dc)]
    if hist is None:
        assert seq % tm == 0 and tm % SUBLANES == 0
        period = None
        ins.append(u)
        specs.append(pl.BlockSpec((SUBLANES, dc), lambda i: (jnp.maximum(i * (tm // SUBLANES) - 1, 0), 0)))
    else:
        assert tm % seq == 0
        period = seq
        ins += list(hist)
        specs += [row(dc), row(dc)]
    cwp = jnp.pad(cw, ((0, SUBLANES - cw.shape[0]), (0, 0)))
    ins += [sga, sgb, cwp, wa, wb, wo, gf]
    specs += [row(d), row(d)] + [_const_spec(a.shape) for a in (cwp, wa, wb, wo, gf)]
    if mode == "dense":
        ins += list(ffn)
        specs += [_const_spec(a.shape) for a in ffn]
        if final_g is not None:
            ins.append(final_g)
            specs.append(_const_spec(final_g.shape))
        out_specs = row(d)
        out_shape = jax.ShapeDtypeStruct((n, d), F32)
        n_exp = 0
    else:
        router = ffn
        n_exp = router.shape[1]
        ep = -(-n_exp // SUBLANES) * SUBLANES
        rt = jnp.pad(router.T, ((0, ep - n_exp), (0, 0)))
        ti = lax.broadcasted_iota(I32, (tm, tm), 0)
        tj = lax.broadcasted_iota(I32, (tm, tm), 1)
        tri = jnp.where(ti < tj, 1.0, 0.0).astype(BF16)
        ins += [rt, tri]
        specs += [_const_spec(rt.shape), _const_spec(tri.shape)]
        lane_rows = lambda: pl.BlockSpec((SUBLANES, tm), lambda i: (0, i))
        out_specs = [row(d), row(d), lane_rows(), lane_rows(), pl.BlockSpec((None, ep, LANES), lambda i: (i, 0, 0))]
        out_shape = [jax.ShapeDtypeStruct((n, d), F32), jax.ShapeDtypeStruct((n, d), BF16),
                     jax.ShapeDtypeStruct((SUBLANES, n), I32), jax.ShapeDtypeStruct((SUBLANES, n), F32),
                     jax.ShapeDtypeStruct((nt, ep, LANES), F32)]
    return pl.pallas_call(
        functools.partial(_mix_out_kernel, seq=seq, period=period, mode=mode,
                          final_norm=final_g is not None, n_exp=n_exp),
        grid=(nt,),
        in_specs=specs,
        out_specs=out_specs,
        out_shape=out_shape,
        compiler_params=_params(("parallel",)),
        name="mix_out_" + mode,
    )(*ins)


def _dispatch_kernel(ii_ref, ij_ref, fl_ref, r0_ref, r1_ref, n_ref, x_ref, pos_ref, rg_ref, xs_ref, gs_ref,
                     *, ts, win):
    w = pl.program_id(0)

    @pl.when(w < n_ref[0])
    def _():
        tt = x_ref.shape[0]

        @pl.when(fl_ref[w] == 1)
        def _():
            xs_ref[...] = jnp.zeros_like(xs_ref)
            gs_ref[...] = jnp.zeros_like(gs_ref)

        for k in range(ts // win):
            @pl.when((r0_ref[w] < (k + 1) * win) & (r1_ref[w] > k * win))
            def _():
                rows = pl.ds(k * win, win)
                r = ii_ref[w] * ts + k * win + lax.broadcasted_iota(I32, (win, tt), 0)
                m1 = r == pos_ref[0:1, :]
                m2 = r == pos_ref[1:2, :]
                sel = jnp.where(m1 | m2, 1.0, 0.0).astype(BF16)
                xs = jnp.dot(sel, x_ref[...], preferred_element_type=F32).astype(BF16)
                gate = jnp.sum(jnp.where(m1, rg_ref[0:1, :], 0.0) + jnp.where(m2, rg_ref[1:2, :], 0.0),
                               axis=1, keepdims=True)
                xs_ref[rows, :] += xs
                gs_ref[rows, :] += jnp.broadcast_to(gate, (win, gs_ref.shape[1]))


def _dispatch(xn, pos, rg, items, n_items, ts, tt, rows):
    n, d = xn.shape
    ii, ij, first, r0, r1 = items
    win = _pick(ts, (WIN,))
    tok = lambda w, ii, ij, *_: (ij[w], 0)
    tok_t = lambda w, ii, ij, *_: (0, ij[w])
    srt = lambda w, ii, *_: (ii[w], 0)
    gs = pltpu.PrefetchScalarGridSpec(
        num_scalar_prefetch=6,
        grid=(ii.shape[0],),
        in_specs=[pl.BlockSpec((tt, d), tok), pl.BlockSpec((SUBLANES, tt), tok_t),
                  pl.BlockSpec((SUBLANES, tt), tok_t)],
        out_specs=[pl.BlockSpec((ts, d), srt), pl.BlockSpec((ts, LANES), srt)],
    )
    return pl.pallas_call(
        functools.partial(_dispatch_kernel, ts=ts, win=win),
        grid_spec=gs,
        out_shape=[jax.ShapeDtypeStruct((rows, d), BF16), jax.ShapeDtypeStruct((rows, LANES), F32)],
        compiler_params=_params(("arbitrary",)),
        name="moe_dispatch",
    )(ii, ij, first, r0, r1, n_items, xn, pos, rg)


def _experts_kernel(te_ref, nv_ref, xs_ref, wg_ref, wu_ref, wd_ref, gs_ref, y_ref, acc_ref):
    i = pl.program_id(0)
    c = pl.program_id(1)

    @pl.when(i < nv_ref[0])
    def _():
        x = xs_ref[...]
        hg = jnp.dot(x, wg_ref[...], preferred_element_type=F32)
        hu = jnp.dot(x, wu_ref[...], preferred_element_type=F32)
        hh = (hg * _sigmoid(hg) * hu).astype(BF16)
        part = jnp.dot(hh, wd_ref[...], preferred_element_type=F32)

        @pl.when(c == 0)
        def _():
            acc_ref[...] = part

        @pl.when(c > 0)
        def _():
            acc_ref[...] += part

        @pl.when(c == pl.num_programs(1) - 1)
        def _():
            y_ref[...] = (acc_ref[...] * gs_ref[:, 0:1]).astype(BF16)


def _experts(xs, gsort, wg, wu, wd, tile_expert, n_valid, ts):
    rows, d = xs.shape
    dff = wg.shape[2]
    fc = _pick(dff, (2048, 1792, 1536, 1280, 1024, 896, 768, 640, 512, 384, 256, 128))
    nfc = dff // fc
    tile = lambda i, nv: jnp.maximum(jnp.minimum(i, nv[0] - 1), 0)
    chunk = lambda i, c, nv: jnp.where(i < nv[0], c, nfc - 1)
    gs = pltpu.PrefetchScalarGridSpec(
        num_scalar_prefetch=2,
        grid=(rows // ts, nfc),
        in_specs=[pl.BlockSpec((ts, d), lambda i, c, te, nv: (tile(i, nv), 0)),
                  pl.BlockSpec((None, d, fc), lambda i, c, te, nv: (te[i], 0, chunk(i, c, nv))),
                  pl.BlockSpec((None, d, fc), lambda i, c, te, nv: (te[i], 0, chunk(i, c, nv))),
                  pl.BlockSpec((None, fc, d), lambda i, c, te, nv: (te[i], chunk(i, c, nv), 0)),
                  pl.BlockSpec((ts, LANES), lambda i, c, te, nv: (tile(i, nv), 0))],
        out_specs=pl.BlockSpec((ts, d), lambda i, c, te, nv: (tile(i, nv), 0)),
        scratch_shapes=[pltpu.VMEM((ts, d), F32)],
    )
    return pl.pallas_call(
        _experts_kernel,
        grid_spec=gs,
        out_shape=jax.ShapeDtypeStruct((rows, d), BF16),
        compiler_params=_params(("arbitrary", "arbitrary")),
        name="moe_experts",
    )(tile_expert, n_valid, xs, wg, wu, wd, gsort)


def _combine_kernel(ii_ref, ij_ref, fl_ref, r0_ref, r1_ref, n_ref, x1_ref, ys_ref, posc_ref, gn_ref, y_ref,
                    *, ts, win, final_norm):
    w = pl.program_id(0)

    @pl.when(w < n_ref[0])
    def _():
        tt = x1_ref.shape[0]
        fl = fl_ref[w]

        @pl.when(fl % 2 == 1)
        def _():
            y_ref[...] = x1_ref[...]

        for k in range(ts // win):
            @pl.when((r0_ref[w] < (k + 1) * win) & (r1_ref[w] > k * win))
            def _():
                r = ii_ref[w] * ts + k * win + lax.broadcasted_iota(I32, (tt, win), 1)
                sel = jnp.where((r == posc_ref[:, 0:1]) | (r == posc_ref[:, 1:2]), 1.0, 0.0).astype(BF16)
                y_ref[...] += jnp.dot(sel, ys_ref[pl.ds(k * win, win), :], preferred_element_type=F32)

        if final_norm:
            @pl.when(fl >= 2)
            def _():
                y_ref[...] = _rms(y_ref[...], gn_ref[...])


def _combine(x1, ys, posc, items, n_items, final_g, ts, tt):
    n, d = x1.shape
    ii, ij, flags, r0, r1 = items
    win = _pick(ts, (MXU_DIM,))
    gn = final_g if final_g is not None else jnp.ones((1, d), F32)
    tok = lambda w, ii, ij, *_: (ij[w], 0)
    gs = pltpu.PrefetchScalarGridSpec(
        num_scalar_prefetch=6,
        grid=(ii.shape[0],),
        in_specs=[pl.BlockSpec((tt, d), tok),
                  pl.BlockSpec((ts, d), lambda w, ii, *_: (ii[w], 0)),
                  pl.BlockSpec((tt, LANES), tok),
                  pl.BlockSpec((1, d), lambda w, *_: (0, 0))],
        out_specs=pl.BlockSpec((tt, d), tok),
    )
    return pl.pallas_call(
        functools.partial(_combine_kernel, ts=ts, win=win, final_norm=final_g is not None),
        grid_spec=gs,
        out_shape=jax.ShapeDtypeStruct((n, d), F32),
        compiler_params=_params(("arbitrary",)),
        name="moe_combine",
    )(ii, ij, flags, r0, r1, n_items, x1, ys, posc, gn)


def _route_tables(ri, cnt, n_exp, tt, ts):
    n = ri.shape[1]
    nj = n // tt
    cnt = cnt[:, :n_exp, 0].astype(I32)
    tot = jnp.sum(cnt, axis=0)
    gsize = (tot + ts - 1) // ts * ts
    gend = jnp.cumsum(gsize)
    goff = gend - gsize
    seg_start = goff[None, :] + jnp.cumsum(cnt, axis=0) - cnt
    e1, e2, w1, w2 = ri[0], ri[1], ri[2], ri[3]
    oh = lambda e: e[:, None] == jnp.arange(n_exp, dtype=I32)[None, :]
    base = jnp.repeat(seg_start, tt, axis=0)
    pos1 = jnp.sum(jnp.where(oh(e1), base, 0), axis=1) + w1
    pos2 = jnp.sum(jnp.where(oh(e2), base, 0), axis=1) + w2
    pos = jnp.zeros((SUBLANES, n), I32).at[0].set(pos1).at[1].set(pos2)
    posc = jnp.zeros((n, LANES), I32).at[:, 0].set(pos1).at[:, 1].set(pos2)

    valid = cnt > 0
    i0 = seg_start // ts
    i1 = (seg_start + cnt - 1) // ts
    jj = jnp.broadcast_to(jnp.arange(nj, dtype=I32)[:, None], cnt.shape)
    it_i = jnp.concatenate([i0.reshape(-1), i1.reshape(-1)])
    it_j = jnp.concatenate([jj.reshape(-1), jj.reshape(-1)])
    it_v = jnp.concatenate([valid.reshape(-1), (valid & (i1 > i0)).reshape(-1)])
    seg_end = seg_start + cnt
    it_r0 = jnp.concatenate([(seg_start - i0 * ts).reshape(-1), jnp.zeros_like(i1).reshape(-1)])
    it_r1 = jnp.concatenate([(jnp.minimum(seg_end, (i0 + 1) * ts) - i0 * ts).reshape(-1),
                             (seg_end - i1 * ts).reshape(-1)])
    n_items = jnp.sum(it_v).astype(I32).reshape(1)
    big = jnp.iinfo(jnp.int32).max
    ni = n * TOP_K // ts + n_exp

    def ordered(key):
        order = jnp.argsort(jnp.where(it_v, key, big))
        last = jnp.maximum(n_items[0] - 1, 0)
        idx = jnp.minimum(jnp.arange(order.shape[0]), last)
        pick = order[idx]
        live = jnp.arange(order.shape[0]) < n_items[0]
        return it_i[pick], it_j[pick], live, pick

    si, sj, live, pick = ordered(it_i * nj + it_j)
    first = jnp.concatenate([jnp.ones((1,), bool), si[1:] != si[:-1]]) & live
    disp = (si, sj, first.astype(I32), it_r0[pick], it_r1[pick])
    ci, cj, live, cpick = ordered(it_j * ni + it_i)
    cfirst = jnp.concatenate([jnp.ones((1,), bool), cj[1:] != cj[:-1]]) & live
    nxt_live = jnp.concatenate([live[1:], jnp.zeros((1,), bool)])
    clast = (jnp.concatenate([cj[1:] != cj[:-1], jnp.ones((1,), bool)]) | ~nxt_live) & live
    comb = (ci, cj, cfirst.astype(I32) + 2 * clast.astype(I32), it_r0[cpick], it_r1[cpick])

    n_valid = (gend[-1] // ts).astype(I32).reshape(1)
    tstart = jnp.arange(ni, dtype=I32) * ts
    te = jnp.sum(tstart[:, None] >= gend[None, :], axis=1).astype(I32)
    te_last = jnp.sum((n_valid[0] - 1) * ts >= gend).astype(I32)
    te = jnp.where(jnp.arange(ni) < n_valid[0], jnp.minimum(te, n_exp - 1), te_last)
    return pos, posc, disp, comb, n_items, te, n_valid, ni * ts


def _moe(x1, xn, ri, rg, cnt, wg, wu, wd, final_g, tt):
    n_exp = wg.shape[0]
    ts = tt
    pos, posc, disp, comb, n_items, te, n_valid, rows = _route_tables(ri, cnt, n_exp, tt, ts)
    xs, gsort = _dispatch(xn, pos, rg, disp, n_items, ts, tt, rows)
    ys = _experts(xs, gsort, wg, wu, wd, te, n_valid, ts)
    return _combine(x1, ys, posc, comb, n_items, final_g, ts, tt)


def kernel(x_prompt, x_sample, cache_k, cache_v, state_conv, page_table, norm_mix, w_in, sb_bias, w_a,
           conv_w, w_b, w_o, norm_ffn, ffn_wg, ffn_wu, ffn_wd, router, moe_wg, moe_wu, moe_wd, norm_final):
    b, s, d = x_prompt.shape
    db, t, _ = x_sample.shape
    depth = w_in.shape[0]
    sbw = w_a.shape[1]
    dc = w_b.shape[1]
    h = sb_bias.shape[1]
    hd = sbw // h
    scale = hd ** -0.5 * LOG2E
    ckt = jnp.transpose(cache_k, (0, 1, 3, 4, 2))
    cvt = jnp.transpose(cache_v, (0, 1, 3, 4, 2))
    row2 = lambda g: g.reshape(1, d)

    xp = x_prompt.reshape(b * s, d)
    xs = x_sample.reshape(db * t, d)
    tm_p = _pick(s, (512, 256, 128))
    tm_s = db * t
    cp_l, ks_l, vs_l, cs_l = [], [], [], []
    kv_all = None
    trow = jnp.arange(db * t) % t
    for l in range(depth):
        win = w_in[l].astype(BF16)
        wkv_t = w_in[l][:, sbw:3 * sbw].T.astype(BF16)
        wa, wb, wo = w_a[l].astype(BF16), w_b[l].astype(BF16), w_o[l].astype(BF16)
        last = l == depth - 1
        final_g = row2(norm_final) if last else None
        fi = l // 2
        if l % 2 == 0:
            mode = "dense"
            ffn = (ffn_wg[fi].astype(BF16), ffn_wu[fi].astype(BF16), ffn_wd[fi].astype(BF16))
        else:
            mode = "moe"
            ffn = router[fi]
            ewg, ewu, ewd = moe_wg[fi].astype(BF16), moe_wu[fi].astype(BF16), moe_wd[fi].astype(BF16)

        q, kt_all, vt_all, ktb, vtb, bg, u, sga, sgb = _in_proj(
            xp, row2(norm_mix[l]), win, wkv_t, b, s, sbw, dc, scale, layer=l, depth=depth, kv_all=kv_all)
        kv_all = (kt_all, vt_all)
        o = _sb_prompt(q, ktb, vtb, sb_bias[l], hd)
        res = _mix_out(xp, o, bg, u, sga, sgb, conv_w[l], wa, wb, wo, row2(norm_ffn[l]), seq=s, hist=None,
                       mode=mode, ffn=ffn, final_g=final_g if mode == "dense" else None, tm=tm_p)
        xp = res if mode == "dense" else _moe(*res, ewg, ewu, ewd, final_g, tm_p)
        cp_l.append(u.reshape(b, s, dc)[:, s - (conv_w.shape[1] - 1):])

        q, kt, vt, _, _, bg, u, sga, sgb = _in_proj(xs, row2(norm_mix[l]), win, wkv_t, 1, db * t, sbw, dc, scale)
        kt, vt = kt[0], vt[0]
        o = _sb_sample(q.reshape(db, t, sbw), kt[0], vt[0], sb_bias[l], ckt, cvt, l, page_table, hd)
        st = state_conv[l]
        h1 = jnp.repeat(st[:, 1], t, axis=0)
        h2 = jnp.where((trow == 0)[:, None], jnp.repeat(st[:, 0], t, axis=0), h1)
        res = _mix_out(xs, o.reshape(db * t, sbw), bg, u, sga, sgb, conv_w[l], wa, wb, wo, row2(norm_ffn[l]),
                       seq=t, hist=(h1, h2), mode=mode, ffn=ffn,
                       final_g=final_g if mode == "dense" else None, tm=tm_s)
        xs = res if mode == "dense" else _moe(*res, ewg, ewu, ewd, final_g, tm_s)
        ks_l.append(kt[0].T.reshape(db, t, h, hd))
        vs_l.append(vt[0].T.reshape(db, t, h, hd))
        cs_l.append(jnp.concatenate([st, u.reshape(db, t, dc)], axis=1)[:, t:])

    seq_major = lambda a: jnp.transpose(a.reshape(depth, b, h, hd, s), (0, 1, 4, 2, 3))
    return (xp.reshape(b, s, d), xs.reshape(db, t, d), seq_major(kv_all[0]), seq_major(kv_all[1]), jnp.stack(cp_l),
            jnp.stack(ks_l), jnp.stack(vs_l), jnp.stack(cs_l))
```

```python
import functools

import jax
import jax.numpy as jnp
from jax import lax
from jax.experimental import pallas as pl
from jax.experimental.pallas import tpu as pltpu

F32 = jnp.float32
BF16 = jnp.bfloat16
I32 = jnp.int32

RMS_EPS = 1e-6
TOP_K = 2
LANES = 128
SUBLANES = 8
BF16_ROWS = 16
MXU_DIM = 256
LOG2E = 1.4426950408889634
BIAS_TERMS = 3
WIN = 256
MOE_COARSE = 1024
ROW_BITS = 10
VMEM_LIMIT = 56 * 1024 * 1024


def _pick(n, cands):
    for c in cands:
        if n % c == 0:
            return c
    return n


def _params(sem):
    return pltpu.CompilerParams(dimension_semantics=sem, vmem_limit_bytes=VMEM_LIMIT)


def _const_spec(shape):
    nd = len(shape)
    return pl.BlockSpec(shape, lambda *_: (0,) * nd, pipeline_mode=pl.Buffered(1))


def _rms(x, g):
    return x * lax.rsqrt(jnp.mean(x * x, axis=-1, keepdims=True) + RMS_EPS) * g


def _sigmoid(x):
    return 1.0 / (1.0 + jnp.exp(-x))


def _in_proj_kernel(x_ref, g_ref, w_ref, wkv_ref, *rest, sbw, dc, d, scale):
    q_ref, kt_ref, vt_ref, ktb_ref, vtb_ref, bg_ref, u_ref, sga_ref, sgb_ref = rest[-9:]
    xn = _rms(x_ref[...], g_ref[...]).astype(BF16)

    def proj(lo, n):
        return jnp.dot(xn, w_ref[:, lo:lo + n], preferred_element_type=F32)

    def proj_t(lo):
        return lax.dot_general(wkv_ref[lo:lo + sbw, :], xn, (((1,), (1,)), ((), ())),
                               preferred_element_type=F32)

    q_ref[...] = (proj(0, sbw) * scale).astype(BF16)
    kt = proj_t(0)
    kt_ref[...] = kt
    ktb_ref[...] = kt.astype(BF16)
    vt = proj_t(sbw)
    vt_ref[...] = vt
    vtb_ref[...] = vt.astype(BF16)
    o = 3 * sbw
    bg_ref[...] = proj(o, dc).astype(BF16)
    u_ref[...] = proj(o + dc, dc) * proj(o + 2 * dc, dc)
    o += 3 * dc
    sga_ref[...] = _sigmoid(proj(o, d)).astype(BF16)
    sgb_ref[...] = _sigmoid(proj(o + d, d)).astype(BF16)


def _in_proj(x, g, w, wkv_t, b, s, sbw, dc, scale, layer=0, depth=1, kv_all=None):
    n, d = x.shape
    tm = _pick(s, (512, 256, 128))
    nt = s // tm
    row = lambda c: pl.BlockSpec((tm, c), lambda i: (i, 0))
    col = lambda: pl.BlockSpec((None, sbw, tm), lambda i: (i // nt, 0, i % nt))
    kv = lambda: pl.BlockSpec((None, None, sbw, tm), lambda i: (layer, i // nt, 0, i % nt))
    outs = [(row(sbw), (n, sbw), BF16), (kv(), (depth, b, sbw, s), F32), (kv(), (depth, b, sbw, s), F32),
            (col(), (b, sbw, s), BF16), (col(), (b, sbw, s), BF16), (row(dc), (n, dc), BF16),
            (row(dc), (n, dc), F32), (row(d), (n, d), BF16), (row(d), (n, d), BF16)]
    ins = [x, g, w, wkv_t]
    specs = [row(d), _const_spec((1, d)), _const_spec(w.shape), _const_spec(wkv_t.shape)]
    aliases = {}
    if kv_all is not None:
        aliases = {len(ins): 1, len(ins) + 1: 2}
        ins += list(kv_all)
        specs += [pl.BlockSpec(memory_space=pl.ANY)] * 2
    return pl.pallas_call(
        functools.partial(_in_proj_kernel, sbw=sbw, dc=dc, d=d, scale=scale),
        grid=(n // tm,),
        in_specs=specs,
        out_specs=[o[0] for o in outs],
        out_shape=[jax.ShapeDtypeStruct(o[1], o[2]) for o in outs],
        input_output_aliases=aliases,
        compiler_params=_params(("parallel",)),
        name="in_proj",
    )(*ins)


def _sb_scores(q, kt, bias):
    z = jnp.dot(q, kt, preferred_element_type=F32)
    if bias is not None:
        z = z + bias
    sp = jnp.maximum(z, 0.0) + jnp.log(1.0 + jnp.exp2(-jnp.abs(z))) * LOG2E
    return sp.astype(BF16), z


def _sb_suffix(sp, nt, mask):
    if mask is not None:
        sp = jnp.where(mask, sp, jnp.zeros_like(sp))
    e = jnp.dot(sp, nt, preferred_element_type=F32)
    return e, e[:, :1]


def _sb_weights(z, e, c, mask):
    aw = jnp.exp2(z + e + c)
    if mask is not None:
        aw = jnp.where(mask, aw, 0.0)
    return aw.astype(BF16)


def _sb_pv(aw, vt):
    return lax.dot_general(aw, vt, (((1,), (1,)), ((), ())), preferred_element_type=F32)


def _sb_chunk(q, kt, vt, nt, bias, c, mask):
    sub = nt.shape[0]
    sp, z = _sb_scores(q, kt, bias)
    aws = []
    for blk in reversed(range(kt.shape[1] // sub)):
        sl = slice(blk * sub, (blk + 1) * sub)
        m = None if mask is None else mask[:, sl]
        e, tot = _sb_suffix(sp[:, sl], nt, m)
        aws.append(_sb_weights(z[:, sl], e, c, m))
        c = c + tot
    aw = jnp.concatenate(aws[::-1], axis=1) if len(aws) > 1 else aws[0]
    return c, _sb_pv(aw, vt)


def _sb_prompt_kernel(bias_ref, q_ref, kt_ref, vt_ref, nt_ref, o_ref, acc_ref, *, tq, hd, hpg, ng):
    g = pl.program_id(1)
    i = pl.program_id(2)
    w = hpg * hd
    nt = nt_ref[...]
    lane = lax.broadcasted_iota(I32, (tq, w), 1)
    row = lax.broadcasted_iota(I32, (tq, tq), 0)
    col = lax.broadcasted_iota(I32, (tq, tq), 1)
    causal = col < row
    in_head = [(lane >= a * hd) & (lane < (a + 1) * hd) for a in range(hpg)]
    qs = []
    for gi in range(ng):
        q2 = q_ref[:, gi * w:(gi + 1) * w].astype(F32)
        for a in range(hpg):
            head = (g * ng + gi) * hpg + a
            ext = jnp.zeros((tq, w), F32)
            for k in range(BIAS_TERMS):
                ext = jnp.where(lane == k, bias_ref[head * BIAS_TERMS + k], ext)
            qs.append(jnp.concatenate([jnp.where(in_head[a], q2, 0.0), ext], axis=1).astype(BF16))
    ones_rows = jnp.where(lax.broadcasted_iota(I32, (w, tq), 0) < BIAS_TERMS, 1.0, 0.0).astype(BF16)

    sub = nt.shape[0]

    def chunk(j, cs, diagonal):
        keys = pl.ds(pl.multiple_of(j * tq, tq), tq)
        out = []
        for gi in range(ng):
            kt = jnp.concatenate([kt_ref[gi * w:(gi + 1) * w, keys], ones_rows], axis=0)
            vt = vt_ref[gi * w:(gi + 1) * w, keys]
            for a in range(hpg):
                k = gi * hpg + a
                if diagonal:
                    parts = []
                    for rb in range(tq // sub):
                        rows = slice(rb * sub, (rb + 1) * sub)
                        nk = (rb + 1) * sub
                        c, pv = _sb_chunk(qs[k][rows], kt[:, :nk], vt[:, :nk], nt, None,
                                          jnp.zeros((sub, 1), F32), causal[rows, :nk])
                        acc_ref[k, rows, :] = pv
                        parts.append(c)
                    out.append(jnp.concatenate(parts, axis=0))
                else:
                    c, pv = _sb_chunk(qs[k], kt, vt, nt, None, cs[k], None)
                    acc_ref[k] += pv
                    out.append(c)
        return tuple(out)

    cs = chunk(i, None, True)
    lax.fori_loop(0, i, lambda jj, cs: chunk(i - 1 - jj, cs, False), cs)
    for gi in range(ng):
        out = acc_ref[gi * hpg]
        for a in range(1, hpg):
            out = jnp.where(in_head[a], acc_ref[gi * hpg + a], out)
        o_ref[:, gi * w:(gi + 1) * w] = out.astype(BF16)


def _split_bf16(x):
    terms = []
    for _ in range(BIAS_TERMS):
        t = x.astype(BF16).astype(F32)
        terms.append(t)
        x = x - t
    return jnp.stack(terms, axis=-1)


def _neg_tri(t):
    j = lax.broadcasted_iota(I32, (t, t), 0)
    s = lax.broadcasted_iota(I32, (t, t), 1)
    return jnp.where(j >= s, -1.0, 0.0).astype(BF16)


def _sb_prompt(q, ktb, vtb, bias, hd):
    b, sbw, s = ktb.shape
    n = q.shape[0]
    hpg = max(1, LANES // hd)
    w = hpg * hd
    ng = _pick(sbw // w, (2, 1))
    gw = ng * w
    tq = _pick(s, (512, 256, 128))
    sub = _pick(tq, (MXU_DIM, LANES))
    nq = s // tq
    return pl.pallas_call(
        functools.partial(_sb_prompt_kernel, tq=tq, hd=hd, hpg=hpg, ng=ng),
        grid=(b, sbw // gw, nq),
        in_specs=[pl.BlockSpec(memory_space=pltpu.SMEM),
                  pl.BlockSpec((tq, gw), lambda bi, g, i: (bi * nq + i, g)),
                  pl.BlockSpec((None, gw, s), lambda bi, g, i: (bi, g, 0)),
                  pl.BlockSpec((None, gw, s), lambda bi, g, i: (bi, g, 0)),
                  _const_spec((sub, sub))],
        out_specs=pl.BlockSpec((tq, gw), lambda bi, g, i: (bi * nq + i, g)),
        out_shape=jax.ShapeDtypeStruct((n, sbw), BF16),
        scratch_shapes=[pltpu.VMEM((ng * hpg, tq, w), F32)],
        compiler_params=_params(("parallel", "parallel", "arbitrary")),
        name="sb_prompt",
    )(_split_bf16(bias * LOG2E).reshape(-1), q, ktb, vtb, _neg_tri(sub))


def _sb_sample_kernel(pt_ref, q_ref, bias_ref, kn_ref, vn_ref, ntn_ref, nt_ref, *rest, pg, t, h, hd):
    k_refs = rest[:pg]
    v_refs = rest[pg:2 * pg]
    o_ref = rest[2 * pg]
    c_ref, acc_ref = rest[2 * pg + 1:]
    bi = pl.program_id(0)
    s = pl.program_id(1)
    sbw = h * hd
    rows = t * h
    bias = bias_ref[...]
    q = q_ref[...].astype(F32)
    qrows = jnp.concatenate([jnp.broadcast_to(q[i:i + 1], (h, sbw)) for i in range(t)], axis=0)
    r = lax.broadcasted_iota(I32, (rows, sbw), 0)
    cidx = lax.broadcasted_iota(I32, (rows, sbw), 1)
    headmask = (cidx // hd) == (r % h)
    qbd = jnp.where(headmask, qrows, 0.0).astype(BF16)

    @pl.when(s == 0)
    def _():
        ns = kn_ref.shape[1]
        rq = lax.broadcasted_iota(I32, (rows, ns), 0) // h
        ck = lax.broadcasted_iota(I32, (rows, ns), 1)
        mask = (ck // t == bi) & (ck % t < rq)
        c, pv = _sb_chunk(qbd, kn_ref[...].astype(BF16), vn_ref[...].astype(BF16), ntn_ref[...], bias,
                          jnp.zeros((rows, 1), F32), mask)
        c_ref[...] = c
        acc_ref[...] = pv

    nt = nt_ref[...]
    page = nt.shape[0]
    scores = [_sb_scores(qbd, k_refs[i][...].reshape(sbw, page).astype(BF16), bias) for i in range(pg)]
    sums = [_sb_suffix(sp, nt, None) for sp, _ in scores]
    c = c_ref[...]
    pv = acc_ref[...]
    for i in reversed(range(pg)):
        aw = _sb_weights(scores[i][1], sums[i][0], c, None)
        pv = pv + _sb_pv(aw, v_refs[i][...].reshape(sbw, page).astype(BF16))
        c = c + sums[i][1]
    c_ref[...] = c
    acc_ref[...] = pv

    @pl.when(s == pl.num_programs(1) - 1)
    def _():
        acc = jnp.where(headmask, acc_ref[...], 0.0)
        o_ref[...] = jnp.sum(acc.reshape(t, h, sbw), axis=1).astype(BF16)


def _sb_sample(q, kt_new, vt_new, bias, cache_kt, cache_vt, layer, page_table, hd):
    db, t, sbw = q.shape
    h = sbw // hd
    page = cache_kt.shape[4]
    ns = kt_new.shape[1]
    n_pages = page_table.shape[1]
    pg = _pick(n_pages, (16, 8, 4, 2, 1))
    n_steps = n_pages // pg
    rows = t * h
    bias_rows = (jnp.tile(bias, t) * LOG2E).reshape(rows, 1)

    def page_spec(i):
        def imap(bi, s, pt):
            return (layer, pt[bi * n_pages + (n_steps - 1 - s) * pg + i], 0, 0, 0)
        return pl.BlockSpec((None, None, h, hd, page), imap)

    const = lambda shape: pl.BlockSpec(shape, lambda bi, s, pt: (0,) * len(shape))
    qspec = pl.BlockSpec((None, t, sbw), lambda bi, s, pt: (bi, 0, 0))
    gs = pltpu.PrefetchScalarGridSpec(
        num_scalar_prefetch=1,
        grid=(db, n_steps),
        in_specs=[qspec, const((rows, 1)), const((sbw, ns)), const((sbw, ns)), const((ns, ns)),
                  const((page, page))] + [page_spec(i) for i in range(pg)] * 2,
        out_specs=qspec,
        scratch_shapes=[pltpu.VMEM((rows, 1), F32), pltpu.VMEM((rows, sbw), F32)],
    )
    return pl.pallas_call(
        functools.partial(_sb_sample_kernel, pg=pg, t=t, h=h, hd=hd),
        grid_spec=gs,
        out_shape=jax.ShapeDtypeStruct((db, t, sbw), BF16),
        compiler_params=_params(("parallel", "arbitrary")),
        name="sb_sample",
    )(page_table.reshape(-1), q, bias_rows, kt_new, vt_new, _neg_tri(ns), _neg_tri(page),
      *([cache_kt] * pg), *([cache_vt] * pg))


def _mix_out_kernel(*refs, seq, period, mode, final_norm, n_exp):
    it = iter(refs)
    x_ref, o_ref, bg_ref, u_ref = next(it), next(it), next(it), next(it)
    if period is None:
        uprev_ref = next(it)
    else:
        h1_ref, h2_ref = next(it), next(it)
    sga_ref, sgb_ref, cw_ref, wa_ref, wb_ref, wo_ref, gf_ref = (next(it) for _ in range(7))
    if mode == "dense":
        wg_ref, wu_ref, wd_ref = next(it), next(it), next(it)
        if final_norm:
            gn_ref = next(it)
        y_ref = next(it)
    else:
        rt_ref, tri_ref = next(it), next(it)
        x1_ref, xn_ref, ri_ref, rg_ref, cnt_ref = (next(it) for _ in range(5))

    tm = x_ref.shape[0]
    u = u_ref[...]
    row = lax.broadcasted_iota(I32, u.shape, 0)
    u1 = pltpu.roll(u, 1, axis=0)
    u2 = pltpu.roll(u, 2, axis=0)
    if period is None:
        first = (pl.program_id(0) % (seq // tm)) == 0
        keep = jnp.where(first, 0.0, 1.0)
        hm1 = uprev_ref[SUBLANES - 1:SUBLANES, :] * keep
        hm2 = uprev_ref[SUBLANES - 2:SUBLANES - 1, :] * keep
        u1 = jnp.where(row == 0, hm1, u1)
        u2 = jnp.where(row == 0, hm2, jnp.where(row == 1, hm1, u2))
    else:
        u1 = jnp.where(row % period == 0, h1_ref[...], u1)
        u2 = jnp.where(row % period < 2, h2_ref[...], u2)
    cv = cw_ref[0:1, :] * u2 + cw_ref[1:2, :] * u1 + cw_ref[2:3, :] * u

    ya = jnp.dot(o_ref[...], wa_ref[...], preferred_element_type=F32)
    yb = jnp.dot((bg_ref[...].astype(F32) * cv).astype(BF16), wb_ref[...], preferred_element_type=F32)
    mix = (sga_ref[...].astype(F32) * ya + sgb_ref[...].astype(F32) * yb).astype(BF16)
    x1 = x_ref[...] + jnp.dot(mix, wo_ref[...], preferred_element_type=F32)
    xnf = _rms(x1, gf_ref[...])
    xn = xnf.astype(BF16)

    if mode == "dense":
        hg = jnp.dot(xn, wg_ref[...], preferred_element_type=F32)
        hu = jnp.dot(xn, wu_ref[...], preferred_element_type=F32)
        hh = (hg * _sigmoid(hg) * hu).astype(BF16)
        y = x1 + jnp.dot(hh, wd_ref[...], preferred_element_type=F32)
        if final_norm:
            y = _rms(y, gn_ref[...])
        y_ref[...] = y
        return

    x1_ref[...] = x1
    xn_ref[...] = xn
    ep = rt_ref.shape[0]
    rt = rt_ref[...]
    rh = rt.astype(BF16)
    rl = (rt - rh.astype(F32)).astype(BF16)
    xl = (xnf - xn.astype(F32)).astype(BF16)
    dn = (((1,), (1,)), ((), ()))
    logits = (lax.dot_general(rh, xn, dn, preferred_element_type=F32)
              + lax.dot_general(rl, xn, dn, preferred_element_type=F32)
              + lax.dot_general(rh, xl, dn, preferred_element_type=F32))
    ie = lax.broadcasted_iota(I32, (ep, tm), 0)
    logits = jnp.where(ie < n_exp, logits, -jnp.inf)
    m1 = jnp.max(logits, axis=0, keepdims=True)
    i1 = jnp.min(jnp.where(logits == m1, ie, ep), axis=0, keepdims=True)
    l2 = jnp.where(ie == i1, -jnp.inf, logits)
    m2 = jnp.max(l2, axis=0, keepdims=True)
    i2 = jnp.min(jnp.where(l2 == m2, ie, ep), axis=0, keepdims=True)
    e2 = jnp.exp(m2 - m1)
    g1 = 1.0 / (1.0 + e2)
    g2 = e2 / (1.0 + e2)
    oh = jnp.where((ie == i1) | (ie == i2), 1.0, 0.0)
    before = jnp.dot(oh.astype(BF16), tri_ref[...], preferred_element_type=F32)
    w1 = jnp.sum(jnp.where(ie == i1, before, 0.0), axis=0, keepdims=True).astype(I32)
    w2 = jnp.sum(jnp.where(ie == i2, before, 0.0), axis=0, keepdims=True).astype(I32)
    r8 = lax.broadcasted_iota(I32, (SUBLANES, tm), 0)
    ri_ref[...] = jnp.where(r8 == 0, i1, jnp.where(r8 == 1, i2, jnp.where(r8 == 2, w1, jnp.where(r8 == 3, w2, 0))))
    rg_ref[...] = jnp.where(r8 == 0, g1, jnp.where(r8 == 1, g2, 0.0))
    cnt_ref[...] = jnp.broadcast_to(jnp.sum(oh, axis=1, keepdims=True), (ep, LANES))


def _mix_out(x, o, bg, u, sga, sgb, cw, wa, wb, wo, gf, *, seq, hist, mode, ffn, final_g, tm):
    n, d = x.shape
    sbw, dc = o.shape[1], bg.shape[1]
    nt = n // tm
    row = lambda c: pl.BlockSpec((tm, c), lambda i: (i, 0))
    ins = [x, o, bg, u]
    specs = [row(d), row(sbw), row(dc), row(dc)]
    if hist is None:
        assert seq % tm == 0 and tm % SUBLANES == 0
        period = None
        ins.append(u)
        specs.append(pl.BlockSpec((SUBLANES, dc), lambda i: (jnp.maximum(i * (tm // SUBLANES) - 1, 0), 0)))
    else:
        assert tm % seq == 0
        period = seq
        ins += list(hist)
        specs += [row(dc), row(dc)]
    cwp = jnp.pad(cw, ((0, SUBLANES - cw.shape[0]), (0, 0)))
    ins += [sga, sgb, cwp, wa, wb, wo, gf]
    specs += [row(d), row(d)] + [_const_spec(a.shape) for a in (cwp, wa, wb, wo, gf)]
    if mode == "dense":
        ins += list(ffn)
        specs += [_const_spec(a.shape) for a in ffn]
        if final_g is not None:
            ins.append(final_g)
            specs.append(_const_spec(final_g.shape))
        out_specs = row(d)
        out_shape = jax.ShapeDtypeStruct((n, d), F32)
        n_exp = 0
    else:
        router = ffn
        n_exp = router.shape[1]
        ep = -(-n_exp // BF16_ROWS) * BF16_ROWS
        rt = jnp.pad(router.T, ((0, ep - n_exp), (0, 0)))
        ti = lax.broadcasted_iota(I32, (tm, tm), 0)
        tj = lax.broadcasted_iota(I32, (tm, tm), 1)
        tri = jnp.where(ti < tj, 1.0, 0.0).astype(BF16)
        ins += [rt, tri]
        specs += [_const_spec(rt.shape), _const_spec(tri.shape)]
        lane_rows = lambda: pl.BlockSpec((SUBLANES, tm), lambda i: (0, i))
        out_specs = [row(d), row(d), lane_rows(), lane_rows(), pl.BlockSpec((None, ep, LANES), lambda i: (i, 0, 0))]
        out_shape = [jax.ShapeDtypeStruct((n, d), F32), jax.ShapeDtypeStruct((n, d), BF16),
                     jax.ShapeDtypeStruct((SUBLANES, n), I32), jax.ShapeDtypeStruct((SUBLANES, n), F32),
                     jax.ShapeDtypeStruct((nt, ep, LANES), F32)]
    return pl.pallas_call(
        functools.partial(_mix_out_kernel, seq=seq, period=period, mode=mode,
                          final_norm=final_g is not None, n_exp=n_exp),
        grid=(nt,),
        in_specs=specs,
        out_specs=out_specs,
        out_shape=out_shape,
        compiler_params=_params(("parallel",)),
        name="mix_out_" + mode,
    )(*ins)


def _dispatch_kernel(ii_ref, ij_ref, fl_ref, r0_ref, r1_ref, n_ref, x_ref, pos_ref, rg_ref, xs_ref, gs_ref,
                     *, ts, win):
    w = pl.program_id(0)

    @pl.when(w < n_ref[0])
    def _():
        tt = x_ref.shape[0]

        @pl.when(fl_ref[w] == 1)
        def _():
            xs_ref[...] = jnp.zeros_like(xs_ref)
            gs_ref[...] = jnp.zeros_like(gs_ref)

        for k in range(ts // win):
            @pl.when((r0_ref[w] < (k + 1) * win) & (r1_ref[w] > k * win))
            def _():
                rows = pl.ds(k * win, win)
                r = ii_ref[w] * ts + k * win + lax.broadcasted_iota(I32, (win, tt), 0)
                m1 = r == pos_ref[0:1, :]
                m2 = r == pos_ref[1:2, :]
                sel = jnp.where(m1 | m2, 1.0, 0.0).astype(BF16)
                xs = jnp.dot(sel, x_ref[...], preferred_element_type=F32).astype(BF16)
                gate = jnp.sum(jnp.where(m1, rg_ref[0:1, :], 0.0) + jnp.where(m2, rg_ref[1:2, :], 0.0),
                               axis=1, keepdims=True)
                xs_ref[rows, :] += xs
                gs_ref[rows, :] += jnp.broadcast_to(gate, (win, gs_ref.shape[1]))


def _dispatch(xn, pos, rg, items, n_items, ts, tt, rows):
    n, d = xn.shape
    ii, ij, first, r0, r1 = items
    win = _pick(ts, (WIN,))
    tok = lambda w, ii, ij, *_: (ij[w], 0)
    tok_t = lambda w, ii, ij, *_: (0, ij[w])
    srt = lambda w, ii, *_: (ii[w], 0)
    gs = pltpu.PrefetchScalarGridSpec(
        num_scalar_prefetch=6,
        grid=(ii.shape[0],),
        in_specs=[pl.BlockSpec((tt, d), tok), pl.BlockSpec((SUBLANES, tt), tok_t),
                  pl.BlockSpec((SUBLANES, tt), tok_t)],
        out_specs=[pl.BlockSpec((ts, d), srt), pl.BlockSpec((ts, LANES), srt)],
    )
    return pl.pallas_call(
        functools.partial(_dispatch_kernel, ts=ts, win=win),
        grid_spec=gs,
        out_shape=[jax.ShapeDtypeStruct((rows, d), BF16), jax.ShapeDtypeStruct((rows, LANES), F32)],
        compiler_params=_params(("arbitrary",)),
        name="moe_dispatch",
    )(ii, ij, first, r0, r1, n_items, xn, pos, rg)


def _experts_kernel(te_ref, nv_ref, xs_ref, wg_ref, wu_ref, wd_ref, gs_ref, y_ref, acc_ref):
    i = pl.program_id(0)
    c = pl.program_id(1)

    @pl.when(i < nv_ref[0])
    def _():
        x = xs_ref[...]
        hg = jnp.dot(x, wg_ref[...], preferred_element_type=F32)
        hu = jnp.dot(x, wu_ref[...], preferred_element_type=F32)
        hh = (hg * _sigmoid(hg) * hu).astype(BF16)
        part = jnp.dot(hh, wd_ref[...], preferred_element_type=F32)

        @pl.when(c == 0)
        def _():
            acc_ref[...] = part

        @pl.when(c > 0)
        def _():
            acc_ref[...] += part

        @pl.when(c == pl.num_programs(1) - 1)
        def _():
            y_ref[...] = (acc_ref[...] * gs_ref[:, 0:1]).astype(BF16)


def _experts(xs, gsort, wg, wu, wd, tile_expert, n_valid, ts):
    rows, d = xs.shape
    dff = wg.shape[2]
    fc = _pick(dff, (2048, 1792, 1536, 1280, 1024, 896, 768, 640, 512, 384, 256, 128))
    nfc = dff // fc
    tile = lambda i, nv: jnp.maximum(jnp.minimum(i, nv[0] - 1), 0)
    chunk = lambda i, c, nv: jnp.where(i < nv[0], c, nfc - 1)
    gs = pltpu.PrefetchScalarGridSpec(
        num_scalar_prefetch=2,
        grid=(rows // ts, nfc),
        in_specs=[pl.BlockSpec((ts, d), lambda i, c, te, nv: (tile(i, nv), 0)),
                  pl.BlockSpec((None, d, fc), lambda i, c, te, nv: (te[i], 0, chunk(i, c, nv))),
                  pl.BlockSpec((None, d, fc), lambda i, c, te, nv: (te[i], 0, chunk(i, c, nv))),
                  pl.BlockSpec((None, fc, d), lambda i, c, te, nv: (te[i], chunk(i, c, nv), 0)),
                  pl.BlockSpec((ts, LANES), lambda i, c, te, nv: (tile(i, nv), 0))],
        out_specs=pl.BlockSpec((ts, d), lambda i, c, te, nv: (tile(i, nv), 0)),
        scratch_shapes=[pltpu.VMEM((ts, d), F32)],
    )
    return pl.pallas_call(
        _experts_kernel,
        grid_spec=gs,
        out_shape=jax.ShapeDtypeStruct((rows, d), BF16),
        compiler_params=_params(("arbitrary", "arbitrary")),
        name="moe_experts",
    )(tile_expert, n_valid, xs, wg, wu, wd, gsort)


def _combine_kernel(ii_ref, ij_ref, fl_ref, r0_ref, r1_ref, n_ref, x1_ref, ys_ref, posc_ref, gn_ref, y_ref,
                    *, ts, win, final_norm):
    w = pl.program_id(0)

    @pl.when(w < n_ref[0])
    def _():
        tt = x1_ref.shape[0]
        fl = fl_ref[w]

        @pl.when(fl % 2 == 1)
        def _():
            y_ref[...] = x1_ref[...]

        for k in range(ts // win):
            @pl.when((r0_ref[w] < (k + 1) * win) & (r1_ref[w] > k * win))
            def _():
                r = ii_ref[w] * ts + k * win + lax.broadcasted_iota(I32, (tt, win), 1)
                sel = jnp.where((r == posc_ref[:, 0:1]) | (r == posc_ref[:, 1:2]), 1.0, 0.0).astype(BF16)
                y_ref[...] += jnp.dot(sel, ys_ref[pl.ds(k * win, win), :], preferred_element_type=F32)

        if final_norm:
            @pl.when(fl >= 2)
            def _():
                y_ref[...] = _rms(y_ref[...], gn_ref[...])


def _combine(x1, ys, posc, items, n_items, final_g, ts, tt):
    n, d = x1.shape
    ii, ij, flags, r0, r1 = items
    win = _pick(ts, (MXU_DIM,))
    gn = final_g if final_g is not None else jnp.ones((1, d), F32)
    tok = lambda w, ii, ij, *_: (ij[w], 0)
    gs = pltpu.PrefetchScalarGridSpec(
        num_scalar_prefetch=6,
        grid=(ii.shape[0],),
        in_specs=[pl.BlockSpec((tt, d), tok),
                  pl.BlockSpec((ts, d), lambda w, ii, *_: (ii[w], 0)),
                  pl.BlockSpec((tt, LANES), tok),
                  pl.BlockSpec((1, d), lambda w, *_: (0, 0))],
        out_specs=pl.BlockSpec((tt, d), tok),
    )
    return pl.pallas_call(
        functools.partial(_combine_kernel, ts=ts, win=win, final_norm=final_g is not None),
        grid_spec=gs,
        out_shape=jax.ShapeDtypeStruct((n, d), F32),
        compiler_params=_params(("arbitrary",)),
        name="moe_combine",
    )(ii, ij, flags, r0, r1, n_items, x1, ys, posc, gn)


def _route_tables(ri, cnt, n_exp, tt, ts, g):
    n = ri.shape[1]
    nj = n // tt
    cnt = cnt[:, :n_exp, 0].astype(I32)
    tot = jnp.sum(cnt, axis=0)
    gsize = (tot + ts - 1) // ts * ts
    gend = jnp.cumsum(gsize)
    goff = gend - gsize
    seg_start = goff[None, :] + jnp.cumsum(cnt, axis=0) - cnt
    e1, e2, w1, w2 = ri[0], ri[1], ri[2], ri[3]
    oh = lambda e: e[:, None] == jnp.arange(n_exp, dtype=I32)[None, :]
    base = jnp.repeat(seg_start, tt, axis=0)
    pos1 = jnp.sum(jnp.where(oh(e1), base, 0), axis=1) + w1
    pos2 = jnp.sum(jnp.where(oh(e2), base, 0), axis=1) + w2
    pos = jnp.zeros((SUBLANES, n), I32).at[0].set(pos1).at[1].set(pos2)
    posc = jnp.zeros((n, LANES), I32).at[:, 0].set(pos1).at[:, 1].set(pos2)

    rows = -(-(n * TOP_K + n_exp * ts) // g) * g
    gm = g // tt
    njd, nid = nj // gm, rows // g
    s_lo = seg_start.reshape(njd, gm, n_exp)[:, 0]
    s_hi = s_lo + jnp.sum(cnt.reshape(njd, gm, n_exp), axis=1)
    t0 = (jnp.arange(nid, dtype=I32) * g)[:, None, None]
    lo = jnp.maximum(s_lo[None], t0)
    hi = jnp.minimum(s_hi[None], t0 + g)
    hit = hi > lo
    it_v = jnp.any(hit, axis=2)
    it_r0 = jnp.min(jnp.where(hit, lo - t0, g), axis=2)
    it_r1 = jnp.max(jnp.where(hit, hi - t0, 0), axis=2)
    n_items = jnp.sum(it_v).astype(I32).reshape(1)
    w_max = min(nid * njd, n_exp * njd + nid)
    assert nid * njd < 2 ** (31 - 2 * ROW_BITS) and g <= 2 ** ROW_BITS

    def ordered(major, minor, n_minor):
        code = ((major * n_minor + minor) << (2 * ROW_BITS)) | (it_r0 << ROW_BITS) | (it_r1 - 1)
        code = jnp.sort(jnp.where(it_v, code, jnp.iinfo(jnp.int32).max).reshape(-1))[:w_max]
        live = jnp.arange(w_max) < n_items[0]
        code = jnp.where(live, code, code[jnp.maximum(n_items[0] - 1, 0)])
        pair = code >> (2 * ROW_BITS)
        mask = (1 << ROW_BITS) - 1
        return pair // n_minor, pair % n_minor, live, (code >> ROW_BITS) & mask, (code & mask) + 1

    ii = jnp.arange(nid, dtype=I32)[:, None]
    jj = jnp.arange(njd, dtype=I32)[None, :]
    si, sj, live, r0, r1 = ordered(ii, jj, njd)
    first = jnp.concatenate([jnp.ones((1,), bool), si[1:] != si[:-1]]) & live
    disp = (si, sj, first.astype(I32), r0, r1)
    cj, ci, live, r0, r1 = ordered(jj, ii, nid)
    cfirst = jnp.concatenate([jnp.ones((1,), bool), cj[1:] != cj[:-1]]) & live
    nxt_live = jnp.concatenate([live[1:], jnp.zeros((1,), bool)])
    clast = (jnp.concatenate([cj[1:] != cj[:-1], jnp.ones((1,), bool)]) | ~nxt_live) & live
    comb = (ci, cj, cfirst.astype(I32) + 2 * clast.astype(I32), r0, r1)

    ni = rows // ts
    n_valid = (gend[-1] // ts).astype(I32).reshape(1)
    tstart = jnp.arange(ni, dtype=I32) * ts
    te = jnp.sum(tstart[:, None] >= gend[None, :], axis=1).astype(I32)
    te_last = jnp.sum((n_valid[0] - 1) * ts >= gend).astype(I32)
    te = jnp.where(jnp.arange(ni) < n_valid[0], jnp.minimum(te, n_exp - 1), te_last)
    return pos, posc, disp, comb, n_items, te, n_valid, rows


def _moe(x1, xn, ri, rg, cnt, wg, wu, wd, final_g, tt):
    n_exp = wg.shape[0]
    n = x1.shape[0]
    ts = tt
    g = tt * 2 if (n // tt) % 2 == 0 and tt * 2 <= MOE_COARSE else tt
    pos, posc, disp, comb, n_items, te, n_valid, rows = _route_tables(ri, cnt, n_exp, tt, ts, g)
    xs, gsort = _dispatch(xn, pos, rg, disp, n_items, g, g, rows)
    ys = _experts(xs, gsort, wg, wu, wd, te, n_valid, ts)
    return _combine(x1, ys, posc, comb, n_items, final_g, g, g)


def kernel(x_prompt, x_sample, cache_k, cache_v, state_conv, page_table, norm_mix, w_in, sb_bias, w_a,
           conv_w, w_b, w_o, norm_ffn, ffn_wg, ffn_wu, ffn_wd, router, moe_wg, moe_wu, moe_wd, norm_final):
    b, s, d = x_prompt.shape
    db, t, _ = x_sample.shape
    depth = w_in.shape[0]
    sbw = w_a.shape[1]
    dc = w_b.shape[1]
    h = sb_bias.shape[1]
    hd = sbw // h
    scale = hd ** -0.5 * LOG2E
    ckt = jnp.transpose(cache_k, (0, 1, 3, 4, 2))
    cvt = jnp.transpose(cache_v, (0, 1, 3, 4, 2))
    row2 = lambda g: g.reshape(1, d)

    xp = x_prompt.reshape(b * s, d)
    xs = x_sample.reshape(db * t, d)
    tm_p = _pick(s, (512, 256, 128))
    tm_s = db * t
    cp_l, ks_l, vs_l, cs_l = [], [], [], []
    kv_all = None
    trow = jnp.arange(db * t) % t
    for l in range(depth):
        win = w_in[l].astype(BF16)
        wkv_t = w_in[l][:, sbw:3 * sbw].T.astype(BF16)
        wa, wb, wo = w_a[l].astype(BF16), w_b[l].astype(BF16), w_o[l].astype(BF16)
        last = l == depth - 1
        final_g = row2(norm_final) if last else None
        fi = l // 2
        if l % 2 == 0:
            mode = "dense"
            ffn = (ffn_wg[fi].astype(BF16), ffn_wu[fi].astype(BF16), ffn_wd[fi].astype(BF16))
        else:
            mode = "moe"
            ffn = router[fi]
            ewg, ewu, ewd = moe_wg[fi].astype(BF16), moe_wu[fi].astype(BF16), moe_wd[fi].astype(BF16)

        q, kt_all, vt_all, ktb, vtb, bg, u, sga, sgb = _in_proj(
            xp, row2(norm_mix[l]), win, wkv_t, b, s, sbw, dc, scale, layer=l, depth=depth, kv_all=kv_all)
        kv_all = (kt_all, vt_all)
        o = _sb_prompt(q, ktb, vtb, sb_bias[l], hd)
        res = _mix_out(xp, o, bg, u, sga, sgb, conv_w[l], wa, wb, wo, row2(norm_ffn[l]), seq=s, hist=None,
                       mode=mode, ffn=ffn, final_g=final_g if mode == "dense" else None, tm=tm_p)
        xp = res if mode == "dense" else _moe(*res, ewg, ewu, ewd, final_g, tm_p)
        cp_l.append(u.reshape(b, s, dc)[:, s - (conv_w.shape[1] - 1):])

        q, kt, vt, _, _, bg, u, sga, sgb = _in_proj(xs, row2(norm_mix[l]), win, wkv_t, 1, db * t, sbw, dc, scale)
        kt, vt = kt[0], vt[0]
        o = _sb_sample(q.reshape(db, t, sbw), kt[0], vt[0], sb_bias[l], ckt, cvt, l, page_table, hd)
        st = state_conv[l]
        h1 = jnp.repeat(st[:, 1], t, axis=0)
        h2 = jnp.where((trow == 0)[:, None], jnp.repeat(st[:, 0], t, axis=0), h1)
        res = _mix_out(xs, o.reshape(db * t, sbw), bg, u, sga, sgb, conv_w[l], wa, wb, wo, row2(norm_ffn[l]),
                       seq=t, hist=(h1, h2), mode=mode, ffn=ffn,
                       final_g=final_g if mode == "dense" else None, tm=tm_s)
        xs = res if mode == "dense" else _moe(*res, ewg, ewu, ewd, final_g, tm_s)
        ks_l.append(kt[0].T.reshape(db, t, h, hd))
        vs_l.append(vt[0].T.reshape(db, t, h, hd))
        cs_l.append(jnp.concatenate([st, u.reshape(db, t, dc)], axis=1)[:, t:])

    seq_major = lambda a: jnp.transpose(a.reshape(depth, b, h, hd, s), (0, 1, 4, 2, 3))
    return (xp.reshape(b, s, d), xs.reshape(db, t, d), seq_major(kv_all[0]), seq_major(kv_all[1]), jnp.stack(cp_l),
            jnp.stack(ks_l), jnp.stack(vs_l), jnp.stack(cs_l))
```

```python
import functools

import jax
import jax.numpy as jnp
from jax import lax
from jax.experimental import pallas as pl
from jax.experimental.pallas import tpu as pltpu

F32 = jnp.float32
BF16 = jnp.bfloat16
I32 = jnp.int32

RMS_EPS = 1e-6
TOP_K = 2
LANES = 128
SUBLANES = 8
BF16_ROWS = 16
MXU_DIM = 256
LOG2E = 1.4426950408889634
BIAS_TERMS = 3
WIN = 256
MOE_COARSE = 1024
ROW_BITS = 10
VMEM_LIMIT = 56 * 1024 * 1024


def _pick(n, cands):
    for c in cands:
        if n % c == 0:
            return c
    return n


def _params(sem):
    return pltpu.CompilerParams(dimension_semantics=sem, vmem_limit_bytes=VMEM_LIMIT)


def _const_spec(shape):
    nd = len(shape)
    return pl.BlockSpec(shape, lambda *_: (0,) * nd, pipeline_mode=pl.Buffered(1))


def _rms(x, g):
    return x * lax.rsqrt(jnp.mean(x * x, axis=-1, keepdims=True) + RMS_EPS) * g


def _sigmoid(x):
    return 1.0 / (1.0 + jnp.exp(-x))


def _in_proj_kernel(x_ref, g_ref, w_ref, wkv_ref, *rest, sbw, dc, d, scale):
    q_ref, kt_ref, vt_ref, ktb_ref, vtb_ref, bg_ref, u_ref, sga_ref, sgb_ref = rest[-9:]
    xn = _rms(x_ref[...], g_ref[...]).astype(BF16)

    def proj(lo, n):
        return jnp.dot(xn, w_ref[:, lo:lo + n], preferred_element_type=F32)

    def proj_t(lo):
        return lax.dot_general(wkv_ref[lo:lo + sbw, :], xn, (((1,), (1,)), ((), ())),
                               preferred_element_type=F32)

    q_ref[...] = (proj(0, sbw) * scale).astype(BF16)
    kt = proj_t(0)
    kt_ref[...] = kt
    ktb_ref[...] = kt.astype(BF16)
    vt = proj_t(sbw)
    vt_ref[...] = vt
    vtb_ref[...] = vt.astype(BF16)
    o = 3 * sbw
    bg_ref[...] = proj(o, dc).astype(BF16)
    u_ref[...] = proj(o + dc, dc) * proj(o + 2 * dc, dc)
    o += 3 * dc
    sga_ref[...] = _sigmoid(proj(o, d)).astype(BF16)
    sgb_ref[...] = _sigmoid(proj(o + d, d)).astype(BF16)


def _in_proj(x, g, w, wkv_t, b, s, sbw, dc, scale, layer=0, depth=1, kv_all=None):
    n, d = x.shape
    tm = _pick(s, (512, 256, 128))
    nt = s // tm
    row = lambda c: pl.BlockSpec((tm, c), lambda i: (i, 0))
    col = lambda: pl.BlockSpec((None, sbw, tm), lambda i: (i // nt, 0, i % nt))
    kv = lambda: pl.BlockSpec((None, None, sbw, tm), lambda i: (layer, i // nt, 0, i % nt))
    outs = [(row(sbw), (n, sbw), BF16), (kv(), (depth, b, sbw, s), F32), (kv(), (depth, b, sbw, s), F32),
            (col(), (b, sbw, s), BF16), (col(), (b, sbw, s), BF16), (row(dc), (n, dc), BF16),
            (row(dc), (n, dc), F32), (row(d), (n, d), BF16), (row(d), (n, d), BF16)]
    ins = [x, g, w, wkv_t]
    specs = [row(d), _const_spec((1, d)), _const_spec(w.shape), _const_spec(wkv_t.shape)]
    aliases = {}
    if kv_all is not None:
        aliases = {len(ins): 1, len(ins) + 1: 2}
        ins += list(kv_all)
        specs += [pl.BlockSpec(memory_space=pl.ANY)] * 2
    return pl.pallas_call(
        functools.partial(_in_proj_kernel, sbw=sbw, dc=dc, d=d, scale=scale),
        grid=(n // tm,),
        in_specs=specs,
        out_specs=[o[0] for o in outs],
        out_shape=[jax.ShapeDtypeStruct(o[1], o[2]) for o in outs],
        input_output_aliases=aliases,
        compiler_params=_params(("parallel",)),
        name="in_proj",
    )(*ins)


def _sb_scores(q, kt, bias):
    z = jnp.dot(q, kt, preferred_element_type=F32)
    if bias is not None:
        z = z + bias
    sp = jnp.maximum(z, 0.0) + jnp.log(1.0 + jnp.exp2(-jnp.abs(z))) * LOG2E
    return sp.astype(BF16), z


def _sb_suffix(sp, nt, mask):
    if mask is not None:
        sp = jnp.where(mask, sp, jnp.zeros_like(sp))
    e = jnp.dot(sp, nt, preferred_element_type=F32)
    return e, e[:, :1]


def _sb_weights(z, e, c, mask):
    aw = jnp.exp2(z + e + c)
    if mask is not None:
        aw = jnp.where(mask, aw, 0.0)
    return aw.astype(BF16)


def _sb_pv(aw, vt):
    return lax.dot_general(aw, vt, (((1,), (1,)), ((), ())), preferred_element_type=F32)


def _sb_chunk(q, kt, vt, nt, bias, c, mask):
    sub = nt.shape[0]
    sp, z = _sb_scores(q, kt, bias)
    aws = []
    for blk in reversed(range(kt.shape[1] // sub)):
        sl = slice(blk * sub, (blk + 1) * sub)
        m = None if mask is None else mask[:, sl]
        e, tot = _sb_suffix(sp[:, sl], nt, m)
        aws.append(_sb_weights(z[:, sl], e, c, m))
        c = c + tot
    aw = jnp.concatenate(aws[::-1], axis=1) if len(aws) > 1 else aws[0]
    return c, _sb_pv(aw, vt)


def _sb_prompt_kernel(bias_ref, q_ref, kt_ref, vt_ref, nt_ref, o_ref, acc_ref, *, tq, hd, hpg, ng):
    g = pl.program_id(1)
    i = pl.program_id(2)
    w = hpg * hd
    nt = nt_ref[...]
    lane = lax.broadcasted_iota(I32, (tq, w), 1)
    row = lax.broadcasted_iota(I32, (tq, tq), 0)
    col = lax.broadcasted_iota(I32, (tq, tq), 1)
    causal = col < row
    in_head = [(lane >= a * hd) & (lane < (a + 1) * hd) for a in range(hpg)]
    qs = []
    for gi in range(ng):
        q2 = q_ref[:, gi * w:(gi + 1) * w].astype(F32)
        for a in range(hpg):
            head = (g * ng + gi) * hpg + a
            ext = jnp.zeros((tq, w), F32)
            for k in range(BIAS_TERMS):
                ext = jnp.where(lane == k, bias_ref[head * BIAS_TERMS + k], ext)
            qs.append(jnp.concatenate([jnp.where(in_head[a], q2, 0.0), ext], axis=1).astype(BF16))
    ones_rows = jnp.where(lax.broadcasted_iota(I32, (w, tq), 0) < BIAS_TERMS, 1.0, 0.0).astype(BF16)

    sub = nt.shape[0]

    def chunk(j, cs, diagonal):
        keys = pl.ds(pl.multiple_of(j * tq, tq), tq)
        out = []
        for gi in range(ng):
            kt = jnp.concatenate([kt_ref[gi * w:(gi + 1) * w, keys], ones_rows], axis=0)
            vt = vt_ref[gi * w:(gi + 1) * w, keys]
            for a in range(hpg):
                k = gi * hpg + a
                if diagonal:
                    parts = []
                    for rb in range(tq // sub):
                        rows = slice(rb * sub, (rb + 1) * sub)
                        nk = (rb + 1) * sub
                        c, pv = _sb_chunk(qs[k][rows], kt[:, :nk], vt[:, :nk], nt, None,
                                          jnp.zeros((sub, 1), F32), causal[rows, :nk])
                        acc_ref[k, rows, :] = pv
                        parts.append(c)
                    out.append(jnp.concatenate(parts, axis=0))
                else:
                    c, pv = _sb_chunk(qs[k], kt, vt, nt, None, cs[k], None)
                    acc_ref[k] += pv
                    out.append(c)
        return tuple(out)

    cs = chunk(i, None, True)
    lax.fori_loop(0, i, lambda jj, cs: chunk(i - 1 - jj, cs, False), cs)
    for gi in range(ng):
        out = acc_ref[gi * hpg]
        for a in range(1, hpg):
            out = jnp.where(in_head[a], acc_ref[gi * hpg + a], out)
        o_ref[:, gi * w:(gi + 1) * w] = out.astype(BF16)


def _split_bf16(x):
    terms = []
    for _ in range(BIAS_TERMS):
        t = x.astype(BF16).astype(F32)
        terms.append(t)
        x = x - t
    return jnp.stack(terms, axis=-1)


def _neg_tri(t):
    j = lax.broadcasted_iota(I32, (t, t), 0)
    s = lax.broadcasted_iota(I32, (t, t), 1)
    return jnp.where(j >= s, -1.0, 0.0).astype(BF16)


def _sb_prompt(q, ktb, vtb, bias, hd):
    b, sbw, s = ktb.shape
    n = q.shape[0]
    hpg = max(1, LANES // hd)
    w = hpg * hd
    ng = _pick(sbw // w, (4, 2, 1))
    gw = ng * w
    tq = _pick(s, (512, 256, 128))
    sub = _pick(tq, (MXU_DIM, LANES))
    nq = s // tq
    return pl.pallas_call(
        functools.partial(_sb_prompt_kernel, tq=tq, hd=hd, hpg=hpg, ng=ng),
        grid=(b, sbw // gw, nq),
        in_specs=[pl.BlockSpec(memory_space=pltpu.SMEM),
                  pl.BlockSpec((tq, gw), lambda bi, g, i: (bi * nq + i, g)),
                  pl.BlockSpec((None, gw, s), lambda bi, g, i: (bi, g, 0)),
                  pl.BlockSpec((None, gw, s), lambda bi, g, i: (bi, g, 0)),
                  _const_spec((sub, sub))],
        out_specs=pl.BlockSpec((tq, gw), lambda bi, g, i: (bi * nq + i, g)),
        out_shape=jax.ShapeDtypeStruct((n, sbw), BF16),
        scratch_shapes=[pltpu.VMEM((ng * hpg, tq, w), F32)],
        compiler_params=_params(("parallel", "parallel", "arbitrary")),
        name="sb_prompt",
    )(_split_bf16(bias * LOG2E).reshape(-1), q, ktb, vtb, _neg_tri(sub))


def _sb_sample_kernel(pt_ref, q_ref, bias_ref, kn_ref, vn_ref, ntn_ref, nt_ref, *rest, pg, t, h, hd):
    k_refs = rest[:pg]
    v_refs = rest[pg:2 * pg]
    o_ref = rest[2 * pg]
    c_ref, acc_ref = rest[2 * pg + 1:]
    bi = pl.program_id(0)
    s = pl.program_id(1)
    sbw = h * hd
    rows = t * h
    bias = bias_ref[...]
    q = q_ref[...].astype(F32)
    qrows = jnp.concatenate([jnp.broadcast_to(q[i:i + 1], (h, sbw)) for i in range(t)], axis=0)
    r = lax.broadcasted_iota(I32, (rows, sbw), 0)
    cidx = lax.broadcasted_iota(I32, (rows, sbw), 1)
    headmask = (cidx // hd) == (r % h)
    qbd = jnp.where(headmask, qrows, 0.0).astype(BF16)

    @pl.when(s == 0)
    def _():
        ns = kn_ref.shape[1]
        rq = lax.broadcasted_iota(I32, (rows, ns), 0) // h
        ck = lax.broadcasted_iota(I32, (rows, ns), 1)
        mask = (ck // t == bi) & (ck % t < rq)
        c, pv = _sb_chunk(qbd, kn_ref[...].astype(BF16), vn_ref[...].astype(BF16), ntn_ref[...], bias,
                          jnp.zeros((rows, 1), F32), mask)
        c_ref[...] = c
        acc_ref[...] = pv

    nt = nt_ref[...]
    page = nt.shape[0]
    scores = [_sb_scores(qbd, k_refs[i][...].reshape(sbw, page).astype(BF16), bias) for i in range(pg)]
    sums = [_sb_suffix(sp, nt, None) for sp, _ in scores]
    c = c_ref[...]
    pv = acc_ref[...]
    for i in reversed(range(pg)):
        aw = _sb_weights(scores[i][1], sums[i][0], c, None)
        pv = pv + _sb_pv(aw, v_refs[i][...].reshape(sbw, page).astype(BF16))
        c = c + sums[i][1]
    c_ref[...] = c
    acc_ref[...] = pv

    @pl.when(s == pl.num_programs(1) - 1)
    def _():
        acc = jnp.where(headmask, acc_ref[...], 0.0)
        o_ref[...] = jnp.sum(acc.reshape(t, h, sbw), axis=1).astype(BF16)


def _sb_sample(q, kt_new, vt_new, bias, cache_kt, cache_vt, layer, page_table, hd):
    db, t, sbw = q.shape
    h = sbw // hd
    page = cache_kt.shape[4]
    ns = kt_new.shape[1]
    n_pages = page_table.shape[1]
    pg = _pick(n_pages, (32, 16, 8, 4, 2, 1))
    n_steps = n_pages // pg
    rows = t * h
    bias_rows = (jnp.tile(bias, t) * LOG2E).reshape(rows, 1)

    def page_spec(i):
        def imap(bi, s, pt):
            return (layer, pt[bi * n_pages + (n_steps - 1 - s) * pg + i], 0, 0, 0)
        return pl.BlockSpec((None, None, h, hd, page), imap)

    const = lambda shape: pl.BlockSpec(shape, lambda bi, s, pt: (0,) * len(shape))
    qspec = pl.BlockSpec((None, t, sbw), lambda bi, s, pt: (bi, 0, 0))
    gs = pltpu.PrefetchScalarGridSpec(
        num_scalar_prefetch=1,
        grid=(db, n_steps),
        in_specs=[qspec, const((rows, 1)), const((sbw, ns)), const((sbw, ns)), const((ns, ns)),
                  const((page, page))] + [page_spec(i) for i in range(pg)] * 2,
        out_specs=qspec,
        scratch_shapes=[pltpu.VMEM((rows, 1), F32), pltpu.VMEM((rows, sbw), F32)],
    )
    return pl.pallas_call(
        functools.partial(_sb_sample_kernel, pg=pg, t=t, h=h, hd=hd),
        grid_spec=gs,
        out_shape=jax.ShapeDtypeStruct((db, t, sbw), BF16),
        compiler_params=_params(("parallel", "arbitrary")),
        name="sb_sample",
    )(page_table.reshape(-1), q, bias_rows, kt_new, vt_new, _neg_tri(ns), _neg_tri(page),
      *([cache_kt] * pg), *([cache_vt] * pg))


def _mix_out_kernel(*refs, seq, period, mode, final_norm, n_exp):
    it = iter(refs)
    x_ref, o_ref, bg_ref, u_ref = next(it), next(it), next(it), next(it)
    if period is None:
        uprev_ref = next(it)
    else:
        h1_ref, h2_ref = next(it), next(it)
    sga_ref, sgb_ref, cw_ref, wa_ref, wb_ref, wo_ref, gf_ref = (next(it) for _ in range(7))
    if mode == "dense":
        wg_ref, wu_ref, wd_ref = next(it), next(it), next(it)
        if final_norm:
            gn_ref = next(it)
        y_ref = next(it)
    else:
        rt_ref, tri_ref = next(it), next(it)
        x1_ref, xn_ref, ri_ref, rg_ref, cnt_ref = (next(it) for _ in range(5))

    tm = x_ref.shape[0]
    u = u_ref[...]
    row = lax.broadcasted_iota(I32, u.shape, 0)
    u1 = pltpu.roll(u, 1, axis=0)
    u2 = pltpu.roll(u, 2, axis=0)
    if period is None:
        first = (pl.program_id(0) % (seq // tm)) == 0
        keep = jnp.where(first, 0.0, 1.0)
        hm1 = uprev_ref[SUBLANES - 1:SUBLANES, :] * keep
        hm2 = uprev_ref[SUBLANES - 2:SUBLANES - 1, :] * keep
        u1 = jnp.where(row == 0, hm1, u1)
        u2 = jnp.where(row == 0, hm2, jnp.where(row == 1, hm1, u2))
    else:
        u1 = jnp.where(row % period == 0, h1_ref[...], u1)
        u2 = jnp.where(row % period < 2, h2_ref[...], u2)
    cv = cw_ref[0:1, :] * u2 + cw_ref[1:2, :] * u1 + cw_ref[2:3, :] * u

    ya = jnp.dot(o_ref[...], wa_ref[...], preferred_element_type=F32)
    yb = jnp.dot((bg_ref[...].astype(F32) * cv).astype(BF16), wb_ref[...], preferred_element_type=F32)
    mix = (sga_ref[...].astype(F32) * ya + sgb_ref[...].astype(F32) * yb).astype(BF16)
    x1 = x_ref[...] + jnp.dot(mix, wo_ref[...], preferred_element_type=F32)
    xnf = _rms(x1, gf_ref[...])
    xn = xnf.astype(BF16)

    if mode == "dense":
        hg = jnp.dot(xn, wg_ref[...], preferred_element_type=F32)
        hu = jnp.dot(xn, wu_ref[...], preferred_element_type=F32)
        hh = (hg * _sigmoid(hg) * hu).astype(BF16)
        y = x1 + jnp.dot(hh, wd_ref[...], preferred_element_type=F32)
        if final_norm:
            y = _rms(y, gn_ref[...])
        y_ref[...] = y
        return

    x1_ref[...] = x1
    xn_ref[...] = xn
    ep = rt_ref.shape[0]
    rt = rt_ref[...]
    rh = rt.astype(BF16)
    rl = (rt - rh.astype(F32)).astype(BF16)
    xl = (xnf - xn.astype(F32)).astype(BF16)
    dn = (((1,), (1,)), ((), ()))
    logits = (lax.dot_general(rh, xn, dn, preferred_element_type=F32)
              + lax.dot_general(rl, xn, dn, preferred_element_type=F32)
              + lax.dot_general(rh, xl, dn, preferred_element_type=F32))
    ie = lax.broadcasted_iota(I32, (ep, tm), 0)
    logits = jnp.where(ie < n_exp, logits, -jnp.inf)
    m1 = jnp.max(logits, axis=0, keepdims=True)
    i1 = jnp.min(jnp.where(logits == m1, ie, ep), axis=0, keepdims=True)
    l2 = jnp.where(ie == i1, -jnp.inf, logits)
    m2 = jnp.max(l2, axis=0, keepdims=True)
    i2 = jnp.min(jnp.where(l2 == m2, ie, ep), axis=0, keepdims=True)
    e2 = jnp.exp(m2 - m1)
    g1 = 1.0 / (1.0 + e2)
    g2 = e2 / (1.0 + e2)
    oh = jnp.where((ie == i1) | (ie == i2), 1.0, 0.0)
    before = jnp.dot(oh.astype(BF16), tri_ref[...], preferred_element_type=F32)
    w1 = jnp.sum(jnp.where(ie == i1, before, 0.0), axis=0, keepdims=True).astype(I32)
    w2 = jnp.sum(jnp.where(ie == i2, before, 0.0), axis=0, keepdims=True).astype(I32)
    r8 = lax.broadcasted_iota(I32, (SUBLANES, tm), 0)
    ri_ref[...] = jnp.where(r8 == 0, i1, jnp.where(r8 == 1, i2, jnp.where(r8 == 2, w1, jnp.where(r8 == 3, w2, 0))))
    rg_ref[...] = jnp.where(r8 == 0, g1, jnp.where(r8 == 1, g2, 0.0))
    cnt_ref[...] = jnp.broadcast_to(jnp.sum(oh, axis=1, keepdims=True), (ep, LANES))


def _mix_out(x, o, bg, u, sga, sgb, cw, wa, wb, wo, gf, *, seq, hist, mode, ffn, final_g, tm):
    n, d = x.shape
    sbw, dc = o.shape[1], bg.shape[1]
    nt = n // tm
    row = lambda c: pl.BlockSpec((tm, c), lambda i: (i, 0))
    ins = [x, o, bg, u]
    specs = [row(d), row(sbw), row(dc), row(dc)]
    if hist is None:
        assert seq % tm == 0 and tm % SUBLANES == 0
        period = None
        ins.append(u)
        specs.append(pl.BlockSpec((SUBLANES, dc), lambda i: (jnp.maximum(i * (tm // SUBLANES) - 1, 0), 0)))
    else:
        assert tm % seq == 0
        period = seq
        ins += list(hist)
        specs += [row(dc), row(dc)]
    cwp = jnp.pad(cw, ((0, SUBLANES - cw.shape[0]), (0, 0)))
    ins += [sga, sgb, cwp, wa, wb, wo, gf]
    specs += [row(d), row(d)] + [_const_spec(a.shape) for a in (cwp, wa, wb, wo, gf)]
    if mode == "dense":
        ins += list(ffn)
        specs += [_const_spec(a.shape) for a in ffn]
        if final_g is not None:
            ins.append(final_g)
            specs.append(_const_spec(final_g.shape))
        out_specs = row(d)
        out_shape = jax.ShapeDtypeStruct((n, d), F32)
        n_exp = 0
    else:
        router = ffn
        n_exp = router.shape[1]
        ep = -(-n_exp // BF16_ROWS) * BF16_ROWS
        rt = jnp.pad(router.T, ((0, ep - n_exp), (0, 0)))
        ti = lax.broadcasted_iota(I32, (tm, tm), 0)
        tj = lax.broadcasted_iota(I32, (tm, tm), 1)
        tri = jnp.where(ti < tj, 1.0, 0.0).astype(BF16)
        ins += [rt, tri]
        specs += [_const_spec(rt.shape), _const_spec(tri.shape)]
        lane_rows = lambda: pl.BlockSpec((SUBLANES, tm), lambda i: (0, i))
        out_specs = [row(d), row(d), lane_rows(), lane_rows(), pl.BlockSpec((None, ep, LANES), lambda i: (i, 0, 0))]
        out_shape = [jax.ShapeDtypeStruct((n, d), F32), jax.ShapeDtypeStruct((n, d), BF16),
                     jax.ShapeDtypeStruct((SUBLANES, n), I32), jax.ShapeDtypeStruct((SUBLANES, n), F32),
                     jax.ShapeDtypeStruct((nt, ep, LANES), F32)]
    return pl.pallas_call(
        functools.partial(_mix_out_kernel, seq=seq, period=period, mode=mode,
                          final_norm=final_g is not None, n_exp=n_exp),
        grid=(nt,),
        in_specs=specs,
        out_specs=out_specs,
        out_shape=out_shape,
        compiler_params=_params(("parallel",)),
        name="mix_out_" + mode,
    )(*ins)


def _windowed(body, r0, r1, size, win):
    half = size // 2
    if half % win or half == 0:
        body(0, size)
        return
    start = jnp.minimum(r0 & (-win), size - half)
    fits = r1 <= start + half

    @pl.when(fits)
    def _():
        body(pl.multiple_of(start, win), half)

    @pl.when(jnp.logical_not(fits))
    def _():
        body(0, size)


def _dispatch_kernel(ii_ref, ij_ref, fl_ref, r0_ref, r1_ref, n_ref, x_ref, pos_ref, rg_ref, xs_ref, gs_ref,
                     *, ts, win):
    w = pl.program_id(0)

    @pl.when(w < n_ref[0])
    def _():
        tt = x_ref.shape[0]

        @pl.when(fl_ref[w] == 1)
        def _():
            xs_ref[...] = jnp.zeros_like(xs_ref)
            gs_ref[...] = jnp.zeros_like(gs_ref)

        def fill(start, size):
            rows = pl.ds(start, size)
            r = ii_ref[w] * ts + start + lax.broadcasted_iota(I32, (size, tt), 0)
            m1 = r == pos_ref[0:1, :]
            m2 = r == pos_ref[1:2, :]
            sel = jnp.where(m1 | m2, 1.0, 0.0).astype(BF16)
            xs = jnp.dot(sel, x_ref[...], preferred_element_type=F32).astype(BF16)
            gate = jnp.sum(jnp.where(m1, rg_ref[0:1, :], 0.0) + jnp.where(m2, rg_ref[1:2, :], 0.0),
                           axis=1, keepdims=True)
            xs_ref[rows, :] += xs
            gs_ref[rows, :] += jnp.broadcast_to(gate, (size, gs_ref.shape[1]))

        _windowed(fill, r0_ref[w], r1_ref[w], ts, win)


def _dispatch(xn, pos, rg, items, n_items, ts, tt, rows):
    n, d = xn.shape
    ii, ij, first, r0, r1 = items
    win = _pick(ts, (WIN,))
    tok = lambda w, ii, ij, *_: (ij[w], 0)
    tok_t = lambda w, ii, ij, *_: (0, ij[w])
    srt = lambda w, ii, *_: (ii[w], 0)
    gs = pltpu.PrefetchScalarGridSpec(
        num_scalar_prefetch=6,
        grid=(ii.shape[0],),
        in_specs=[pl.BlockSpec((tt, d), tok), pl.BlockSpec((SUBLANES, tt), tok_t),
                  pl.BlockSpec((SUBLANES, tt), tok_t)],
        out_specs=[pl.BlockSpec((ts, d), srt), pl.BlockSpec((ts, LANES), srt)],
    )
    return pl.pallas_call(
        functools.partial(_dispatch_kernel, ts=ts, win=win),
        grid_spec=gs,
        out_shape=[jax.ShapeDtypeStruct((rows, d), BF16), jax.ShapeDtypeStruct((rows, LANES), F32)],
        compiler_params=_params(("arbitrary",)),
        name="moe_dispatch",
    )(ii, ij, first, r0, r1, n_items, xn, pos, rg)


def _experts_kernel(te_ref, nv_ref, xs_ref, wg_ref, wu_ref, wd_ref, gs_ref, y_ref, acc_ref):
    i = pl.program_id(0)
    c = pl.program_id(1)

    @pl.when(i < nv_ref[0])
    def _():
        x = xs_ref[...]
        hg = jnp.dot(x, wg_ref[...], preferred_element_type=F32)
        hu = jnp.dot(x, wu_ref[...], preferred_element_type=F32)
        hh = (hg * _sigmoid(hg) * hu).astype(BF16)
        part = jnp.dot(hh, wd_ref[...], preferred_element_type=F32)

        @pl.when(c == 0)
        def _():
            acc_ref[...] = part

        @pl.when(c > 0)
        def _():
            acc_ref[...] += part

        @pl.when(c == pl.num_programs(1) - 1)
        def _():
            y_ref[...] = (acc_ref[...] * gs_ref[:, 0:1]).astype(BF16)


def _experts(xs, gsort, wg, wu, wd, tile_expert, n_valid, ts):
    rows, d = xs.shape
    dff = wg.shape[2]
    fc = _pick(dff, (2048, 1792, 1536, 1280, 1024, 896, 768, 640, 512, 384, 256, 128))
    nfc = dff // fc
    tile = lambda i, nv: jnp.maximum(jnp.minimum(i, nv[0] - 1), 0)
    chunk = lambda i, c, nv: jnp.where(i < nv[0], c, nfc - 1)
    gs = pltpu.PrefetchScalarGridSpec(
        num_scalar_prefetch=2,
        grid=(rows // ts, nfc),
        in_specs=[pl.BlockSpec((ts, d), lambda i, c, te, nv: (tile(i, nv), 0)),
                  pl.BlockSpec((None, d, fc), lambda i, c, te, nv: (te[i], 0, chunk(i, c, nv))),
                  pl.BlockSpec((None, d, fc), lambda i, c, te, nv: (te[i], 0, chunk(i, c, nv))),
                  pl.BlockSpec((None, fc, d), lambda i, c, te, nv: (te[i], chunk(i, c, nv), 0)),
                  pl.BlockSpec((ts, LANES), lambda i, c, te, nv: (tile(i, nv), 0))],
        out_specs=pl.BlockSpec((ts, d), lambda i, c, te, nv: (tile(i, nv), 0)),
        scratch_shapes=[pltpu.VMEM((ts, d), F32)],
    )
    return pl.pallas_call(
        _experts_kernel,
        grid_spec=gs,
        out_shape=jax.ShapeDtypeStruct((rows, d), BF16),
        compiler_params=_params(("arbitrary", "arbitrary")),
        name="moe_experts",
    )(tile_expert, n_valid, xs, wg, wu, wd, gsort)


def _combine_kernel(ii_ref, ij_ref, fl_ref, r0_ref, r1_ref, n_ref, x1_ref, ys_ref, posc_ref, gn_ref, y_ref,
                    *, ts, win, final_norm):
    w = pl.program_id(0)

    @pl.when(w < n_ref[0])
    def _():
        tt = x1_ref.shape[0]
        fl = fl_ref[w]

        @pl.when(fl % 2 == 1)
        def _():
            y_ref[...] = x1_ref[...]

        def gather(start, size):
            r = ii_ref[w] * ts + start + lax.broadcasted_iota(I32, (tt, size), 1)
            sel = jnp.where((r == posc_ref[:, 0:1]) | (r == posc_ref[:, 1:2]), 1.0, 0.0).astype(BF16)
            y_ref[...] += jnp.dot(sel, ys_ref[pl.ds(start, size), :], preferred_element_type=F32)

        _windowed(gather, r0_ref[w], r1_ref[w], ts, win)

        if final_norm:
            @pl.when(fl >= 2)
            def _():
                y_ref[...] = _rms(y_ref[...], gn_ref[...])


def _combine(x1, ys, posc, items, n_items, final_g, ts, tt):
    n, d = x1.shape
    ii, ij, flags, r0, r1 = items
    win = _pick(ts, (MXU_DIM,))
    gn = final_g if final_g is not None else jnp.ones((1, d), F32)
    tok = lambda w, ii, ij, *_: (ij[w], 0)
    gs = pltpu.PrefetchScalarGridSpec(
        num_scalar_prefetch=6,
        grid=(ii.shape[0],),
        in_specs=[pl.BlockSpec((tt, d), tok),
                  pl.BlockSpec((ts, d), lambda w, ii, *_: (ii[w], 0)),
                  pl.BlockSpec((tt, LANES), tok),
                  pl.BlockSpec((1, d), lambda w, *_: (0, 0))],
        out_specs=pl.BlockSpec((tt, d), tok),
    )
    return pl.pallas_call(
        functools.partial(_combine_kernel, ts=ts, win=win, final_norm=final_g is not None),
        grid_spec=gs,
        out_shape=jax.ShapeDtypeStruct((n, d), F32),
        compiler_params=_params(("arbitrary",)),
        name="moe_combine",
    )(ii, ij, flags, r0, r1, n_items, x1, ys, posc, gn)


def _route_tables(ri, cnt, n_exp, tt, ts, g):
    n = ri.shape[1]
    nj = n // tt
    cnt = cnt[:, :n_exp, 0].astype(I32)
    tot = jnp.sum(cnt, axis=0)
    gsize = (tot + ts - 1) // ts * ts
    gend = jnp.cumsum(gsize)
    goff = gend - gsize
    seg_start = goff[None, :] + jnp.cumsum(cnt, axis=0) - cnt
    e1, e2, w1, w2 = ri[0], ri[1], ri[2], ri[3]
    oh = lambda e: e[:, None] == jnp.arange(n_exp, dtype=I32)[None, :]
    base = jnp.repeat(seg_start, tt, axis=0)
    pos1 = jnp.sum(jnp.where(oh(e1), base, 0), axis=1) + w1
    pos2 = jnp.sum(jnp.where(oh(e2), base, 0), axis=1) + w2
    two = lambda k, a, b: jnp.where(k == 0, a, jnp.where(k == 1, b, 0))
    pos = two(jnp.arange(SUBLANES, dtype=I32)[:, None], pos1[None, :], pos2[None, :])
    posc = two(jnp.arange(LANES, dtype=I32)[None, :], pos1[:, None], pos2[:, None])

    rows = -(-(n * TOP_K + n_exp * ts) // g) * g
    gm = g // tt
    njd, nid = nj // gm, rows // g
    s_lo = seg_start.reshape(njd, gm, n_exp)[:, 0]
    s_hi = s_lo + jnp.sum(cnt.reshape(njd, gm, n_exp), axis=1)
    t0 = (jnp.arange(nid, dtype=I32) * g)[:, None, None]
    lo = jnp.maximum(s_lo[None], t0)
    hi = jnp.minimum(s_hi[None], t0 + g)
    hit = hi > lo
    it_v = jnp.any(hit, axis=2)
    it_r0 = jnp.min(jnp.where(hit, lo - t0, g), axis=2)
    it_r1 = jnp.max(jnp.where(hit, hi - t0, 0), axis=2)
    n_items = jnp.sum(it_v).astype(I32).reshape(1)
    w_max = min(nid * njd, n_exp * njd + nid)
    assert nid * njd < 2 ** (31 - 2 * ROW_BITS) and g <= 2 ** ROW_BITS

    def ordered(major, minor, n_minor):
        code = ((major * n_minor + minor) << (2 * ROW_BITS)) | (it_r0 << ROW_BITS) | (it_r1 - 1)
        code = jnp.sort(jnp.where(it_v, code, jnp.iinfo(jnp.int32).max).reshape(-1))[:w_max]
        live = jnp.arange(w_max) < n_items[0]
        code = jnp.where(live, code, code[jnp.maximum(n_items[0] - 1, 0)])
        pair = code >> (2 * ROW_BITS)
        mask = (1 << ROW_BITS) - 1
        return pair // n_minor, pair % n_minor, live, (code >> ROW_BITS) & mask, (code & mask) + 1

    ii = jnp.arange(nid, dtype=I32)[:, None]
    jj = jnp.arange(njd, dtype=I32)[None, :]
    si, sj, live, r0, r1 = ordered(ii, jj, njd)
    first = jnp.concatenate([jnp.ones((1,), bool), si[1:] != si[:-1]]) & live
    disp = (si, sj, first.astype(I32), r0, r1)
    cj, ci, live, r0, r1 = ordered(jj, ii, nid)
    cfirst = jnp.concatenate([jnp.ones((1,), bool), cj[1:] != cj[:-1]]) & live
    nxt_live = jnp.concatenate([live[1:], jnp.zeros((1,), bool)])
    clast = (jnp.concatenate([cj[1:] != cj[:-1], jnp.ones((1,), bool)]) | ~nxt_live) & live
    comb = (ci, cj, cfirst.astype(I32) + 2 * clast.astype(I32), r0, r1)

    ni = rows // ts
    n_valid = (gend[-1] // ts).astype(I32).reshape(1)
    tstart = jnp.arange(ni, dtype=I32) * ts
    te = jnp.sum(tstart[:, None] >= gend[None, :], axis=1).astype(I32)
    te_last = jnp.sum((n_valid[0] - 1) * ts >= gend).astype(I32)
    te = jnp.where(jnp.arange(ni) < n_valid[0], jnp.minimum(te, n_exp - 1), te_last)
    return pos, posc, disp, comb, n_items, te, n_valid, rows


def _moe(x1, xn, ri, rg, cnt, wg, wu, wd, final_g, tt):
    n_exp = wg.shape[0]
    n = x1.shape[0]
    ts = tt
    g = tt * 2 if (n // tt) % 2 == 0 and tt * 2 <= MOE_COARSE else tt
    pos, posc, disp, comb, n_items, te, n_valid, rows = _route_tables(ri, cnt, n_exp, tt, ts, g)
    xs, gsort = _dispatch(xn, pos, rg, disp, n_items, g, g, rows)
    ys = _experts(xs, gsort, wg, wu, wd, te, n_valid, ts)
    return _combine(x1, ys, posc, comb, n_items, final_g, g, g)


def kernel(x_prompt, x_sample, cache_k, cache_v, state_conv, page_table, norm_mix, w_in, sb_bias, w_a,
           conv_w, w_b, w_o, norm_ffn, ffn_wg, ffn_wu, ffn_wd, router, moe_wg, moe_wu, moe_wd, norm_final):
    b, s, d = x_prompt.shape
    db, t, _ = x_sample.shape
    depth = w_in.shape[0]
    sbw = w_a.shape[1]
    dc = w_b.shape[1]
    h = sb_bias.shape[1]
    hd = sbw // h
    scale = hd ** -0.5 * LOG2E
    ckt = jnp.transpose(cache_k, (0, 1, 3, 4, 2))
    cvt = jnp.transpose(cache_v, (0, 1, 3, 4, 2))
    row2 = lambda g: g.reshape(1, d)

    xp = x_prompt.reshape(b * s, d)
    xs = x_sample.reshape(db * t, d)
    tm_p = _pick(s, (512, 256, 128))
    tm_s = db * t
    cp_l, ks_l, vs_l, cs_l = [], [], [], []
    kv_all = None
    trow = jnp.arange(db * t) % t
    for l in range(depth):
        win = w_in[l].astype(BF16)
        wkv_t = w_in[l][:, sbw:3 * sbw].T.astype(BF16)
        wa, wb, wo = w_a[l].astype(BF16), w_b[l].astype(BF16), w_o[l].astype(BF16)
        last = l == depth - 1
        final_g = row2(norm_final) if last else None
        fi = l // 2
        if l % 2 == 0:
            mode = "dense"
            ffn = (ffn_wg[fi].astype(BF16), ffn_wu[fi].astype(BF16), ffn_wd[fi].astype(BF16))
        else:
            mode = "moe"
            ffn = router[fi]
            ewg, ewu, ewd = moe_wg[fi].astype(BF16), moe_wu[fi].astype(BF16), moe_wd[fi].astype(BF16)

        q, kt_all, vt_all, ktb, vtb, bg, u, sga, sgb = _in_proj(
            xp, row2(norm_mix[l]), win, wkv_t, b, s, sbw, dc, scale, layer=l, depth=depth, kv_all=kv_all)
        kv_all = (kt_all, vt_all)
        o = _sb_prompt(q, ktb, vtb, sb_bias[l], hd)
        res = _mix_out(xp, o, bg, u, sga, sgb, conv_w[l], wa, wb, wo, row2(norm_ffn[l]), seq=s, hist=None,
                       mode=mode, ffn=ffn, final_g=final_g if mode == "dense" else None, tm=tm_p)
        xp = res if mode == "dense" else _moe(*res, ewg, ewu, ewd, final_g, tm_p)
        cp_l.append(u.reshape(b, s, dc)[:, s - (conv_w.shape[1] - 1):])

        q, kt, vt, _, _, bg, u, sga, sgb = _in_proj(xs, row2(norm_mix[l]), win, wkv_t, 1, db * t, sbw, dc, scale)
        kt, vt = kt[0], vt[0]
        o = _sb_sample(q.reshape(db, t, sbw), kt[0], vt[0], sb_bias[l], ckt, cvt, l, page_table, hd)
        st = state_conv[l]
        h1 = jnp.repeat(st[:, 1], t, axis=0)
        h2 = jnp.where((trow == 0)[:, None], jnp.repeat(st[:, 0], t, axis=0), h1)
        res = _mix_out(xs, o.reshape(db * t, sbw), bg, u, sga, sgb, conv_w[l], wa, wb, wo, row2(norm_ffn[l]),
                       seq=t, hist=(h1, h2), mode=mode, ffn=ffn,
                       final_g=final_g if mode == "dense" else None, tm=tm_s)
        xs = res if mode == "dense" else _moe(*res, ewg, ewu, ewd, final_g, tm_s)
        ks_l.append(kt[0].T.reshape(db, t, h, hd))
        vs_l.append(vt[0].T.reshape(db, t, h, hd))
        cs_l.append(jnp.concatenate([st, u.reshape(db, t, dc)], axis=1)[:, t:])

    seq_major = lambda a: jnp.transpose(a.reshape(depth, b, h, hd, s), (0, 1, 4, 2, 3))
    return (xp.reshape(b, s, d), xs.reshape(db, t, d), seq_major(kv_all[0]), seq_major(kv_all[1]), jnp.stack(cp_l),
            jnp.stack(ks_l), jnp.stack(vs_l), jnp.stack(cs_l))
```

```python
import functools

import jax
import jax.numpy as jnp
from jax import lax
from jax.experimental import pallas as pl
from jax.experimental.pallas import tpu as pltpu

F32 = jnp.float32
BF16 = jnp.bfloat16
I32 = jnp.int32

RMS_EPS = 1e-6
TOP_K = 2
LANES = 128
SUBLANES = 8
BF16_ROWS = 16
MXU_DIM = 256
LOG2E = 1.4426950408889634
BIAS_TERMS = 3
WIN = 256
MOE_COARSE = 1024
ROW_BITS = 10
VMEM_LIMIT = 56 * 1024 * 1024


def _pick(n, cands):
    for c in cands:
        if n % c == 0:
            return c
    return n


def _params(sem):
    return pltpu.CompilerParams(dimension_semantics=sem, vmem_limit_bytes=VMEM_LIMIT)


def _const_spec(shape):
    nd = len(shape)
    return pl.BlockSpec(shape, lambda *_: (0,) * nd, pipeline_mode=pl.Buffered(1))


def _rms(x, g):
    return x * lax.rsqrt(jnp.mean(x * x, axis=-1, keepdims=True) + RMS_EPS) * g


def _sigmoid(x):
    return 1.0 / (1.0 + jnp.exp(-x))


def _in_proj_kernel(x_ref, g_ref, w_ref, wkv_ref, *rest, sbw, dc, d, scale):
    q_ref, kt_ref, vt_ref, ktb_ref, vtb_ref, bg_ref, u_ref, sga_ref, sgb_ref = rest[-9:]
    xn = _rms(x_ref[...], g_ref[...]).astype(BF16)

    def proj(lo, n):
        return jnp.dot(xn, w_ref[:, lo:lo + n], preferred_element_type=F32)

    def proj_t(lo):
        return lax.dot_general(wkv_ref[lo:lo + sbw, :], xn, (((1,), (1,)), ((), ())),
                               preferred_element_type=F32)

    q_ref[...] = (proj(0, sbw) * scale).astype(BF16)
    kt = proj_t(0)
    kt_ref[...] = kt
    ktb_ref[...] = kt.astype(BF16)
    vt = proj_t(sbw)
    vt_ref[...] = vt
    vtb_ref[...] = vt.astype(BF16)
    o = 3 * sbw
    bg_ref[...] = proj(o, dc).astype(BF16)
    u_ref[...] = proj(o + dc, dc) * proj(o + 2 * dc, dc)
    o += 3 * dc
    sga_ref[...] = _sigmoid(proj(o, d)).astype(BF16)
    sgb_ref[...] = _sigmoid(proj(o + d, d)).astype(BF16)


def _in_proj(x, g, w, wkv_t, b, s, sbw, dc, scale, layer=0, depth=1, kv_all=None):
    n, d = x.shape
    tm = _pick(s, (512, 256, 128))
    nt = s // tm
    row = lambda c: pl.BlockSpec((tm, c), lambda i: (i, 0))
    col = lambda: pl.BlockSpec((None, sbw, tm), lambda i: (i // nt, 0, i % nt))
    kv = lambda: pl.BlockSpec((None, None, sbw, tm), lambda i: (layer, i // nt, 0, i % nt))
    outs = [(row(sbw), (n, sbw), BF16), (kv(), (depth, b, sbw, s), F32), (kv(), (depth, b, sbw, s), F32),
            (col(), (b, sbw, s), BF16), (col(), (b, sbw, s), BF16), (row(dc), (n, dc), BF16),
            (row(dc), (n, dc), F32), (row(d), (n, d), BF16), (row(d), (n, d), BF16)]
    ins = [x, g, w, wkv_t]
    specs = [row(d), _const_spec((1, d)), _const_spec(w.shape), _const_spec(wkv_t.shape)]
    aliases = {}
    if kv_all is not None:
        aliases = {len(ins): 1, len(ins) + 1: 2}
        ins += list(kv_all)
        specs += [pl.BlockSpec(memory_space=pl.ANY)] * 2
    return pl.pallas_call(
        functools.partial(_in_proj_kernel, sbw=sbw, dc=dc, d=d, scale=scale),
        grid=(n // tm,),
        in_specs=specs,
        out_specs=[o[0] for o in outs],
        out_shape=[jax.ShapeDtypeStruct(o[1], o[2]) for o in outs],
        input_output_aliases=aliases,
        compiler_params=_params(("parallel",)),
        name="in_proj",
    )(*ins)


def _sb_scores(q, kt, bias):
    z = jnp.dot(q, kt, preferred_element_type=F32)
    if bias is not None:
        z = z + bias
    sp = jnp.maximum(z, 0.0) + jnp.log(1.0 + jnp.exp2(-jnp.abs(z))) * LOG2E
    return sp.astype(BF16), z


def _sb_suffix(sp, nt, mask):
    if mask is not None:
        sp = jnp.where(mask, sp, jnp.zeros_like(sp))
    e = jnp.dot(sp, nt, preferred_element_type=F32)
    return e, e[:, :1]


def _sb_weights(z, e, c, mask):
    aw = jnp.exp2(z + e + c)
    if mask is not None:
        aw = jnp.where(mask, aw, 0.0)
    return aw.astype(BF16)


def _sb_pv(aw, vt):
    return lax.dot_general(aw, vt, (((1,), (1,)), ((), ())), preferred_element_type=F32)


def _sb_chunk(q, kt, vt, nt, bias, c, mask):
    sub = nt.shape[0]
    sp, z = _sb_scores(q, kt, bias)
    aws = []
    for blk in reversed(range(kt.shape[1] // sub)):
        sl = slice(blk * sub, (blk + 1) * sub)
        m = None if mask is None else mask[:, sl]
        e, tot = _sb_suffix(sp[:, sl], nt, m)
        aws.append(_sb_weights(z[:, sl], e, c, m))
        c = c + tot
    aw = jnp.concatenate(aws[::-1], axis=1) if len(aws) > 1 else aws[0]
    return c, _sb_pv(aw, vt)


def _sb_prompt_kernel(bias_ref, q_ref, kt_ref, vt_ref, nt_ref, o_ref, acc_ref, *, tq, hd, hpg, ng):
    g = pl.program_id(1)
    i = pl.program_id(2)
    w = hpg * hd
    nt = nt_ref[...]
    lane = lax.broadcasted_iota(I32, (tq, w), 1)
    row = lax.broadcasted_iota(I32, (tq, tq), 0)
    col = lax.broadcasted_iota(I32, (tq, tq), 1)
    causal = col < row
    in_head = [(lane >= a * hd) & (lane < (a + 1) * hd) for a in range(hpg)]
    qs = []
    for gi in range(ng):
        q2 = q_ref[:, gi * w:(gi + 1) * w].astype(F32)
        for a in range(hpg):
            head = (g * ng + gi) * hpg + a
            ext = jnp.zeros((tq, w), F32)
            for k in range(BIAS_TERMS):
                ext = jnp.where(lane == k, bias_ref[head * BIAS_TERMS + k], ext)
            qs.append(jnp.concatenate([jnp.where(in_head[a], q2, 0.0), ext], axis=1).astype(BF16))
    ones_rows = jnp.where(lax.broadcasted_iota(I32, (w, tq), 0) < BIAS_TERMS, 1.0, 0.0).astype(BF16)

    sub = nt.shape[0]

    def chunk(j, cs, diagonal):
        keys = pl.ds(pl.multiple_of(j * tq, tq), tq)
        out = []
        for gi in range(ng):
            kt = jnp.concatenate([kt_ref[gi * w:(gi + 1) * w, keys], ones_rows], axis=0)
            vt = vt_ref[gi * w:(gi + 1) * w, keys]
            for a in range(hpg):
                k = gi * hpg + a
                if diagonal:
                    parts = []
                    for rb in range(tq // sub):
                        rows = slice(rb * sub, (rb + 1) * sub)
                        nk = (rb + 1) * sub
                        c, pv = _sb_chunk(qs[k][rows], kt[:, :nk], vt[:, :nk], nt, None,
                                          jnp.zeros((sub, 1), F32), causal[rows, :nk])
                        acc_ref[k, rows, :] = pv
                        parts.append(c)
                    out.append(jnp.concatenate(parts, axis=0))
                else:
                    c, pv = _sb_chunk(qs[k], kt, vt, nt, None, cs[k], None)
                    acc_ref[k] += pv
                    out.append(c)
        return tuple(out)

    cs = chunk(i, None, True)
    lax.fori_loop(0, i, lambda jj, cs: chunk(i - 1 - jj, cs, False), cs)
    for gi in range(ng):
        out = acc_ref[gi * hpg]
        for a in range(1, hpg):
            out = jnp.where(in_head[a], acc_ref[gi * hpg + a], out)
        o_ref[:, gi * w:(gi + 1) * w] = out.astype(BF16)


def _split_bf16(x):
    terms = []
    for _ in range(BIAS_TERMS):
        t = x.astype(BF16).astype(F32)
        terms.append(t)
        x = x - t
    return jnp.stack(terms, axis=-1)


def _neg_tri(t):
    j = lax.broadcasted_iota(I32, (t, t), 0)
    s = lax.broadcasted_iota(I32, (t, t), 1)
    return jnp.where(j >= s, -1.0, 0.0).astype(BF16)


def _sb_prompt(q, ktb, vtb, bias, hd):
    b, sbw, s = ktb.shape
    n = q.shape[0]
    hpg = max(1, LANES // hd)
    w = hpg * hd
    ng = _pick(sbw // w, (2, 1))
    gw = ng * w
    tq = _pick(s, (1024, 512, 256, 128))
    sub = _pick(tq, (MXU_DIM, LANES))
    nq = s // tq
    return pl.pallas_call(
        functools.partial(_sb_prompt_kernel, tq=tq, hd=hd, hpg=hpg, ng=ng),
        grid=(b, sbw // gw, nq),
        in_specs=[pl.BlockSpec(memory_space=pltpu.SMEM),
                  pl.BlockSpec((tq, gw), lambda bi, g, i: (bi * nq + i, g)),
                  pl.BlockSpec((None, gw, s), lambda bi, g, i: (bi, g, 0)),
                  pl.BlockSpec((None, gw, s), lambda bi, g, i: (bi, g, 0)),
                  _const_spec((sub, sub))],
        out_specs=pl.BlockSpec((tq, gw), lambda bi, g, i: (bi * nq + i, g)),
        out_shape=jax.ShapeDtypeStruct((n, sbw), BF16),
        scratch_shapes=[pltpu.VMEM((ng * hpg, tq, w), F32)],
        compiler_params=_params(("parallel", "parallel", "arbitrary")),
        name="sb_prompt",
    )(_split_bf16(bias * LOG2E).reshape(-1), q, ktb, vtb, _neg_tri(sub))


def _sb_sample_kernel(pt_ref, q_ref, bias_ref, kn_ref, vn_ref, ntn_ref, nt_ref, *rest, pg, t, h, hd):
    k_refs = rest[:pg]
    v_refs = rest[pg:2 * pg]
    o_ref = rest[2 * pg]
    c_ref, acc_ref = rest[2 * pg + 1:]
    bi = pl.program_id(0)
    s = pl.program_id(1)
    sbw = h * hd
    rows = t * h
    bias = bias_ref[...]
    q = q_ref[...].astype(F32)
    qrows = jnp.concatenate([jnp.broadcast_to(q[i:i + 1], (h, sbw)) for i in range(t)], axis=0)
    r = lax.broadcasted_iota(I32, (rows, sbw), 0)
    cidx = lax.broadcasted_iota(I32, (rows, sbw), 1)
    headmask = (cidx // hd) == (r % h)
    qbd = jnp.where(headmask, qrows, 0.0).astype(BF16)

    @pl.when(s == 0)
    def _():
        ns = kn_ref.shape[1]
        rq = lax.broadcasted_iota(I32, (rows, ns), 0) // h
        ck = lax.broadcasted_iota(I32, (rows, ns), 1)
        mask = (ck // t == bi) & (ck % t < rq)
        c, pv = _sb_chunk(qbd, kn_ref[...].astype(BF16), vn_ref[...].astype(BF16), ntn_ref[...], bias,
                          jnp.zeros((rows, 1), F32), mask)
        c_ref[...] = c
        acc_ref[...] = pv

    nt = nt_ref[...]
    page = nt.shape[0]
    scores = [_sb_scores(qbd, k_refs[i][...].reshape(sbw, page).astype(BF16), bias) for i in range(pg)]
    sums = [_sb_suffix(sp, nt, None) for sp, _ in scores]
    c = c_ref[...]
    pv = acc_ref[...]
    for i in reversed(range(pg)):
        aw = _sb_weights(scores[i][1], sums[i][0], c, None)
        pv = pv + _sb_pv(aw, v_refs[i][...].reshape(sbw, page).astype(BF16))
        c = c + sums[i][1]
    c_ref[...] = c
    acc_ref[...] = pv

    @pl.when(s == pl.num_programs(1) - 1)
    def _():
        acc = jnp.where(headmask, acc_ref[...], 0.0)
        o_ref[...] = jnp.sum(acc.reshape(t, h, sbw), axis=1).astype(BF16)


def _sb_sample(q, kt_new, vt_new, bias, cache_kt, cache_vt, layer, page_table, hd):
    db, t, sbw = q.shape
    h = sbw // hd
    page = cache_kt.shape[4]
    ns = kt_new.shape[1]
    n_pages = page_table.shape[1]
    pg = _pick(n_pages, (32, 16, 8, 4, 2, 1))
    n_steps = n_pages // pg
    rows = t * h
    bias_rows = (jnp.tile(bias, t) * LOG2E).reshape(rows, 1)

    def page_spec(i):
        def imap(bi, s, pt):
            return (layer, pt[bi * n_pages + (n_steps - 1 - s) * pg + i], 0, 0, 0)
        return pl.BlockSpec((None, None, h, hd, page), imap)

    const = lambda shape: pl.BlockSpec(shape, lambda bi, s, pt: (0,) * len(shape))
    qspec = pl.BlockSpec((None, t, sbw), lambda bi, s, pt: (bi, 0, 0))
    gs = pltpu.PrefetchScalarGridSpec(
        num_scalar_prefetch=1,
        grid=(db, n_steps),
        in_specs=[qspec, const((rows, 1)), const((sbw, ns)), const((sbw, ns)), const((ns, ns)),
                  const((page, page))] + [page_spec(i) for i in range(pg)] * 2,
        out_specs=qspec,
        scratch_shapes=[pltpu.VMEM((rows, 1), F32), pltpu.VMEM((rows, sbw), F32)],
    )
    return pl.pallas_call(
        functools.partial(_sb_sample_kernel, pg=pg, t=t, h=h, hd=hd),
        grid_spec=gs,
        out_shape=jax.ShapeDtypeStruct((db, t, sbw), BF16),
        compiler_params=_params(("parallel", "arbitrary")),
        name="sb_sample",
    )(page_table.reshape(-1), q, bias_rows, kt_new, vt_new, _neg_tri(ns), _neg_tri(page),
      *([cache_kt] * pg), *([cache_vt] * pg))


def _mix_out_kernel(*refs, seq, period, mode, final_norm, n_exp):
    it = iter(refs)
    x_ref, o_ref, bg_ref, u_ref = next(it), next(it), next(it), next(it)
    if period is None:
        uprev_ref = next(it)
    else:
        h1_ref, h2_ref = next(it), next(it)
    sga_ref, sgb_ref, cw_ref, wa_ref, wb_ref, wo_ref, gf_ref = (next(it) for _ in range(7))
    if mode == "dense":
        wg_ref, wu_ref, wd_ref = next(it), next(it), next(it)
        if final_norm:
            gn_ref = next(it)
        y_ref = next(it)
    else:
        rt_ref, tri_ref = next(it), next(it)
        x1_ref, xn_ref, ri_ref, rg_ref, cnt_ref = (next(it) for _ in range(5))

    tm = x_ref.shape[0]
    u = u_ref[...]
    row = lax.broadcasted_iota(I32, u.shape, 0)
    u1 = pltpu.roll(u, 1, axis=0)
    u2 = pltpu.roll(u, 2, axis=0)
    if period is None:
        first = (pl.program_id(0) % (seq // tm)) == 0
        keep = jnp.where(first, 0.0, 1.0)
        hm1 = uprev_ref[SUBLANES - 1:SUBLANES, :] * keep
        hm2 = uprev_ref[SUBLANES - 2:SUBLANES - 1, :] * keep
        u1 = jnp.where(row == 0, hm1, u1)
        u2 = jnp.where(row == 0, hm2, jnp.where(row == 1, hm1, u2))
    else:
        u1 = jnp.where(row % period == 0, h1_ref[...], u1)
        u2 = jnp.where(row % period < 2, h2_ref[...], u2)
    cv = cw_ref[0:1, :] * u2 + cw_ref[1:2, :] * u1 + cw_ref[2:3, :] * u

    ya = jnp.dot(o_ref[...], wa_ref[...], preferred_element_type=F32)
    yb = jnp.dot((bg_ref[...].astype(F32) * cv).astype(BF16), wb_ref[...], preferred_element_type=F32)
    mix = (sga_ref[...].astype(F32) * ya + sgb_ref[...].astype(F32) * yb).astype(BF16)
    x1 = x_ref[...] + jnp.dot(mix, wo_ref[...], preferred_element_type=F32)
    xnf = _rms(x1, gf_ref[...])
    xn = xnf.astype(BF16)

    if mode == "dense":
        hg = jnp.dot(xn, wg_ref[...], preferred_element_type=F32)
        hu = jnp.dot(xn, wu_ref[...], preferred_element_type=F32)
        hh = (hg * _sigmoid(hg) * hu).astype(BF16)
        y = x1 + jnp.dot(hh, wd_ref[...], preferred_element_type=F32)
        if final_norm:
            y = _rms(y, gn_ref[...])
        y_ref[...] = y
        return

    x1_ref[...] = x1
    xn_ref[...] = xn
    ep = rt_ref.shape[0]
    rt = rt_ref[...]
    rh = rt.astype(BF16)
    rl = (rt - rh.astype(F32)).astype(BF16)
    xl = (xnf - xn.astype(F32)).astype(BF16)
    dn = (((1,), (1,)), ((), ()))
    logits = (lax.dot_general(rh, xn, dn, preferred_element_type=F32)
              + lax.dot_general(rl, xn, dn, preferred_element_type=F32)
              + lax.dot_general(rh, xl, dn, preferred_element_type=F32))
    ie = lax.broadcasted_iota(I32, (ep, tm), 0)
    logits = jnp.where(ie < n_exp, logits, -jnp.inf)
    m1 = jnp.max(logits, axis=0, keepdims=True)
    i1 = jnp.min(jnp.where(logits == m1, ie, ep), axis=0, keepdims=True)
    l2 = jnp.where(ie == i1, -jnp.inf, logits)
    m2 = jnp.max(l2, axis=0, keepdims=True)
    i2 = jnp.min(jnp.where(l2 == m2, ie, ep), axis=0, keepdims=True)
    e2 = jnp.exp(m2 - m1)
    g1 = 1.0 / (1.0 + e2)
    g2 = e2 / (1.0 + e2)
    oh = jnp.where((ie == i1) | (ie == i2), 1.0, 0.0)
    before = jnp.dot(oh.astype(BF16), tri_ref[...], preferred_element_type=F32)
    w1 = jnp.sum(jnp.where(ie == i1, before, 0.0), axis=0, keepdims=True).astype(I32)
    w2 = jnp.sum(jnp.where(ie == i2, before, 0.0), axis=0, keepdims=True).astype(I32)
    r8 = lax.broadcasted_iota(I32, (SUBLANES, tm), 0)
    ri_ref[...] = jnp.where(r8 == 0, i1, jnp.where(r8 == 1, i2, jnp.where(r8 == 2, w1, jnp.where(r8 == 3, w2, 0))))
    rg_ref[...] = jnp.where(r8 == 0, g1, jnp.where(r8 == 1, g2, 0.0))
    cnt_ref[...] = jnp.broadcast_to(jnp.sum(oh, axis=1, keepdims=True), (ep, LANES))


def _mix_out(x, o, bg, u, sga, sgb, cw, wa, wb, wo, gf, *, seq, hist, mode, ffn, final_g, tm):
    n, d = x.shape
    sbw, dc = o.shape[1], bg.shape[1]
    nt = n // tm
    row = lambda c: pl.BlockSpec((tm, c), lambda i: (i, 0))
    ins = [x, o, bg, u]
    specs = [row(d), row(sbw), row(dc), row(dc)]
    if hist is None:
        assert seq % tm == 0 and tm % SUBLANES == 0
        period = None
        ins.append(u)
        specs.append(pl.BlockSpec((SUBLANES, dc), lambda i: (jnp.maximum(i * (tm // SUBLANES) - 1, 0), 0)))
    else:
        assert tm % seq == 0
        period = seq
        ins += list(hist)
        specs += [row(dc), row(dc)]
    cwp = jnp.pad(cw, ((0, SUBLANES - cw.shape[0]), (0, 0)))
    ins += [sga, sgb, cwp, wa, wb, wo, gf]
    specs += [row(d), row(d)] + [_const_spec(a.shape) for a in (cwp, wa, wb, wo, gf)]
    if mode == "dense":
        ins += list(ffn)
        specs += [_const_spec(a.shape) for a in ffn]
        if final_g is not None:
            ins.append(final_g)
            specs.append(_const_spec(final_g.shape))
        out_specs = row(d)
        out_shape = jax.ShapeDtypeStruct((n, d), F32)
        n_exp = 0
    else:
        router = ffn
        n_exp = router.shape[1]
        ep = -(-n_exp // BF16_ROWS) * BF16_ROWS
        rt = jnp.pad(router.T, ((0, ep - n_exp), (0, 0)))
        ti = lax.broadcasted_iota(I32, (tm, tm), 0)
        tj = lax.broadcasted_iota(I32, (tm, tm), 1)
        tri = jnp.where(ti < tj, 1.0, 0.0).astype(BF16)
        ins += [rt, tri]
        specs += [_const_spec(rt.shape), _const_spec(tri.shape)]
        lane_rows = lambda: pl.BlockSpec((SUBLANES, tm), lambda i: (0, i))
        out_specs = [row(d), row(d), lane_rows(), lane_rows(), pl.BlockSpec((None, ep, LANES), lambda i: (i, 0, 0))]
        out_shape = [jax.ShapeDtypeStruct((n, d), F32), jax.ShapeDtypeStruct((n, d), BF16),
                     jax.ShapeDtypeStruct((SUBLANES, n), I32), jax.ShapeDtypeStruct((SUBLANES, n), F32),
                     jax.ShapeDtypeStruct((nt, ep, LANES), F32)]
    return pl.pallas_call(
        functools.partial(_mix_out_kernel, seq=seq, period=period, mode=mode,
                          final_norm=final_g is not None, n_exp=n_exp),
        grid=(nt,),
        in_specs=specs,
        out_specs=out_specs,
        out_shape=out_shape,
        compiler_params=_params(("parallel",)),
        name="mix_out_" + mode,
    )(*ins)


def _windowed(body, r0, r1, size, win):
    half = size // 2
    if half % win or half == 0:
        body(0, size)
        return
    start = jnp.minimum(r0 & (-win), size - half)
    fits = r1 <= start + half

    @pl.when(fits)
    def _():
        body(pl.multiple_of(start, win), half)

    @pl.when(jnp.logical_not(fits))
    def _():
        body(0, size)


def _dispatch_kernel(ii_ref, ij_ref, fl_ref, r0_ref, r1_ref, n_ref, x_ref, pos_ref, rg_ref, xs_ref, gs_ref,
                     *, ts, win):
    w = pl.program_id(0)

    @pl.when(w < n_ref[0])
    def _():
        tt = x_ref.shape[0]

        @pl.when(fl_ref[w] == 1)
        def _():
            xs_ref[...] = jnp.zeros_like(xs_ref)
            gs_ref[...] = jnp.zeros_like(gs_ref)

        def fill(start, size):
            rows = pl.ds(start, size)
            r = ii_ref[w] * ts + start + lax.broadcasted_iota(I32, (size, tt), 0)
            m1 = r == pos_ref[0:1, :]
            m2 = r == pos_ref[1:2, :]
            sel = jnp.where(m1 | m2, 1.0, 0.0).astype(BF16)
            xs = jnp.dot(sel, x_ref[...], preferred_element_type=F32).astype(BF16)
            gate = jnp.sum(jnp.where(m1, rg_ref[0:1, :], 0.0) + jnp.where(m2, rg_ref[1:2, :], 0.0),
                           axis=1, keepdims=True)
            xs_ref[rows, :] += xs
            gs_ref[rows, :] += jnp.broadcast_to(gate, (size, gs_ref.shape[1]))

        _windowed(fill, r0_ref[w], r1_ref[w], ts, win)


def _dispatch(xn, pos, rg, items, n_items, ts, tt, rows):
    n, d = xn.shape
    ii, ij, first, r0, r1 = items
    win = _pick(ts, (WIN,))
    tok = lambda w, ii, ij, *_: (ij[w], 0)
    tok_t = lambda w, ii, ij, *_: (0, ij[w])
    srt = lambda w, ii, *_: (ii[w], 0)
    gs = pltpu.PrefetchScalarGridSpec(
        num_scalar_prefetch=6,
        grid=(ii.shape[0],),
        in_specs=[pl.BlockSpec((tt, d), tok), pl.BlockSpec((SUBLANES, tt), tok_t),
                  pl.BlockSpec((SUBLANES, tt), tok_t)],
        out_specs=[pl.BlockSpec((ts, d), srt), pl.BlockSpec((ts, LANES), srt)],
    )
    return pl.pallas_call(
        functools.partial(_dispatch_kernel, ts=ts, win=win),
        grid_spec=gs,
        out_shape=[jax.ShapeDtypeStruct((rows, d), BF16), jax.ShapeDtypeStruct((rows, LANES), F32)],
        compiler_params=_params(("arbitrary",)),
        name="moe_dispatch",
    )(ii, ij, first, r0, r1, n_items, xn, pos, rg)


def _experts_kernel(te_ref, nv_ref, xs_ref, wg_ref, wu_ref, wd_ref, gs_ref, y_ref, acc_ref):
    i = pl.program_id(0)
    c = pl.program_id(1)

    @pl.when(i < nv_ref[0])
    def _():
        x = xs_ref[...]
        hg = jnp.dot(x, wg_ref[...], preferred_element_type=F32)
        hu = jnp.dot(x, wu_ref[...], preferred_element_type=F32)
        hh = (hg * _sigmoid(hg) * hu).astype(BF16)
        part = jnp.dot(hh, wd_ref[...], preferred_element_type=F32)

        @pl.when(c == 0)
        def _():
            acc_ref[...] = part

        @pl.when(c > 0)
        def _():
            acc_ref[...] += part

        @pl.when(c == pl.num_programs(1) - 1)
        def _():
            y_ref[...] = (acc_ref[...] * gs_ref[:, 0:1]).astype(BF16)


def _experts(xs, gsort, wg, wu, wd, tile_expert, n_valid, ts):
    rows, d = xs.shape
    dff = wg.shape[2]
    fc = _pick(dff, (2048, 1792, 1536, 1280, 1024, 896, 768, 640, 512, 384, 256, 128))
    nfc = dff // fc
    tile = lambda i, nv: jnp.maximum(jnp.minimum(i, nv[0] - 1), 0)
    chunk = lambda i, c, nv: jnp.where(i < nv[0], c, nfc - 1)
    gs = pltpu.PrefetchScalarGridSpec(
        num_scalar_prefetch=2,
        grid=(rows // ts, nfc),
        in_specs=[pl.BlockSpec((ts, d), lambda i, c, te, nv: (tile(i, nv), 0)),
                  pl.BlockSpec((None, d, fc), lambda i, c, te, nv: (te[i], 0, chunk(i, c, nv))),
                  pl.BlockSpec((None, d, fc), lambda i, c, te, nv: (te[i], 0, chunk(i, c, nv))),
                  pl.BlockSpec((None, fc, d), lambda i, c, te, nv: (te[i], chunk(i, c, nv), 0)),
                  pl.BlockSpec((ts, LANES), lambda i, c, te, nv: (tile(i, nv), 0))],
        out_specs=pl.BlockSpec((ts, d), lambda i, c, te, nv: (tile(i, nv), 0)),
        scratch_shapes=[pltpu.VMEM((ts, d), F32)],
    )
    return pl.pallas_call(
        _experts_kernel,
        grid_spec=gs,
        out_shape=jax.ShapeDtypeStruct((rows, d), BF16),
        compiler_params=_params(("arbitrary", "arbitrary")),
        name="moe_experts",
    )(tile_expert, n_valid, xs, wg, wu, wd, gsort)


def _combine_kernel(ii_ref, ij_ref, fl_ref, r0_ref, r1_ref, n_ref, x1_ref, ys_ref, posc_ref, gn_ref, y_ref,
                    *, ts, win, final_norm):
    w = pl.program_id(0)

    @pl.when(w < n_ref[0])
    def _():
        tt = x1_ref.shape[0]
        fl = fl_ref[w]

        @pl.when(fl % 2 == 1)
        def _():
            y_ref[...] = x1_ref[...]

        def gather(start, size):
            r = ii_ref[w] * ts + start + lax.broadcasted_iota(I32, (tt, size), 1)
            sel = jnp.where((r == posc_ref[:, 0:1]) | (r == posc_ref[:, 1:2]), 1.0, 0.0).astype(BF16)
            y_ref[...] += jnp.dot(sel, ys_ref[pl.ds(start, size), :], preferred_element_type=F32)

        _windowed(gather, r0_ref[w], r1_ref[w], ts, win)

        if final_norm:
            @pl.when(fl >= 2)
            def _():
                y_ref[...] = _rms(y_ref[...], gn_ref[...])


def _combine(x1, ys, posc, items, n_items, final_g, ts, tt):
    n, d = x1.shape
    ii, ij, flags, r0, r1 = items
    win = _pick(ts, (MXU_DIM,))
    gn = final_g if final_g is not None else jnp.ones((1, d), F32)
    tok = lambda w, ii, ij, *_: (ij[w], 0)
    gs = pltpu.PrefetchScalarGridSpec(
        num_scalar_prefetch=6,
        grid=(ii.shape[0],),
        in_specs=[pl.BlockSpec((tt, d), tok),
                  pl.BlockSpec((ts, d), lambda w, ii, *_: (ii[w], 0)),
                  pl.BlockSpec((tt, LANES), tok),
                  pl.BlockSpec((1, d), lambda w, *_: (0, 0))],
        out_specs=pl.BlockSpec((tt, d), tok),
    )
    return pl.pallas_call(
        functools.partial(_combine_kernel, ts=ts, win=win, final_norm=final_g is not None),
        grid_spec=gs,
        out_shape=jax.ShapeDtypeStruct((n, d), F32),
        compiler_params=_params(("arbitrary",)),
        name="moe_combine",
    )(ii, ij, flags, r0, r1, n_items, x1, ys, posc, gn)


def _route_tables(ri, cnt, n_exp, tt, ts, g):
    n = ri.shape[1]
    nj = n // tt
    cnt = cnt[:, :n_exp, 0].astype(I32)
    tot = jnp.sum(cnt, axis=0)
    gsize = (tot + ts - 1) // ts * ts
    gend = jnp.cumsum(gsize)
    goff = gend - gsize
    seg_start = goff[None, :] + jnp.cumsum(cnt, axis=0) - cnt
    e1, e2, w1, w2 = ri[0], ri[1], ri[2], ri[3]
    oh = lambda e: e[:, None] == jnp.arange(n_exp, dtype=I32)[None, :]
    base = jnp.repeat(seg_start, tt, axis=0)
    pos1 = jnp.sum(jnp.where(oh(e1), base, 0), axis=1) + w1
    pos2 = jnp.sum(jnp.where(oh(e2), base, 0), axis=1) + w2
    two = lambda k, a, b: jnp.where(k == 0, a, jnp.where(k == 1, b, 0))
    pos = two(jnp.arange(SUBLANES, dtype=I32)[:, None], pos1[None, :], pos2[None, :])
    posc = two(jnp.arange(LANES, dtype=I32)[None, :], pos1[:, None], pos2[:, None])

    rows = -(-(n * TOP_K + n_exp * ts) // g) * g
    gm = g // tt
    njd, nid = nj // gm, rows // g
    s_lo = seg_start.reshape(njd, gm, n_exp)[:, 0]
    s_hi = s_lo + jnp.sum(cnt.reshape(njd, gm, n_exp), axis=1)
    t0 = (jnp.arange(nid, dtype=I32) * g)[:, None, None]
    lo = jnp.maximum(s_lo[None], t0)
    hi = jnp.minimum(s_hi[None], t0 + g)
    hit = hi > lo
    it_v = jnp.any(hit, axis=2)
    it_r0 = jnp.min(jnp.where(hit, lo - t0, g), axis=2)
    it_r1 = jnp.max(jnp.where(hit, hi - t0, 0), axis=2)
    n_items = jnp.sum(it_v).astype(I32).reshape(1)
    w_max = min(nid * njd, n_exp * njd + nid)
    assert nid * njd < 2 ** (31 - 2 * ROW_BITS) and g <= 2 ** ROW_BITS

    def ordered(major, minor, n_minor):
        code = ((major * n_minor + minor) << (2 * ROW_BITS)) | (it_r0 << ROW_BITS) | (it_r1 - 1)
        code = jnp.sort(jnp.where(it_v, code, jnp.iinfo(jnp.int32).max).reshape(-1))[:w_max]
        live = jnp.arange(w_max) < n_items[0]
        code = jnp.where(live, code, code[jnp.maximum(n_items[0] - 1, 0)])
        pair = code >> (2 * ROW_BITS)
        mask = (1 << ROW_BITS) - 1
        return pair // n_minor, pair % n_minor, live, (code >> ROW_BITS) & mask, (code & mask) + 1

    ii = jnp.arange(nid, dtype=I32)[:, None]
    jj = jnp.arange(njd, dtype=I32)[None, :]
    si, sj, live, r0, r1 = ordered(ii, jj, njd)
    first = jnp.concatenate([jnp.ones((1,), bool), si[1:] != si[:-1]]) & live
    disp = (si, sj, first.astype(I32), r0, r1)
    cj, ci, live, r0, r1 = ordered(jj, ii, nid)
    cfirst = jnp.concatenate([jnp.ones((1,), bool), cj[1:] != cj[:-1]]) & live
    nxt_live = jnp.concatenate([live[1:], jnp.zeros((1,), bool)])
    clast = (jnp.concatenate([cj[1:] != cj[:-1], jnp.ones((1,), bool)]) | ~nxt_live) & live
    comb = (ci, cj, cfirst.astype(I32) + 2 * clast.astype(I32), r0, r1)

    ni = rows // ts
    n_valid = (gend[-1] // ts).astype(I32).reshape(1)
    tstart = jnp.arange(ni, dtype=I32) * ts
    te = jnp.sum(tstart[:, None] >= gend[None, :], axis=1).astype(I32)
    te_last = jnp.sum((n_valid[0] - 1) * ts >= gend).astype(I32)
    te = jnp.where(jnp.arange(ni) < n_valid[0], jnp.minimum(te, n_exp - 1), te_last)
    return pos, posc, disp, comb, n_items, te, n_valid, rows


def _moe(x1, xn, ri, rg, cnt, wg, wu, wd, final_g, tt):
    n_exp = wg.shape[0]
    n = x1.shape[0]
    ts = tt
    g = tt * 2 if (n // tt) % 2 == 0 and tt * 2 <= MOE_COARSE else tt
    pos, posc, disp, comb, n_items, te, n_valid, rows = _route_tables(ri, cnt, n_exp, tt, ts, g)
    xs, gsort = _dispatch(xn, pos, rg, disp, n_items, g, g, rows)
    ys = _experts(xs, gsort, wg, wu, wd, te, n_valid, ts)
    return _combine(x1, ys, posc, comb, n_items, final_g, g, g)


def kernel(x_prompt, x_sample, cache_k, cache_v, state_conv, page_table, norm_mix, w_in, sb_bias, w_a,
           conv_w, w_b, w_o, norm_ffn, ffn_wg, ffn_wu, ffn_wd, router, moe_wg, moe_wu, moe_wd, norm_final):
    b, s, d = x_prompt.shape
    db, t, _ = x_sample.shape
    depth = w_in.shape[0]
    sbw = w_a.shape[1]
    dc = w_b.shape[1]
    h = sb_bias.shape[1]
    hd = sbw // h
    scale = hd ** -0.5 * LOG2E
    ckt = jnp.transpose(cache_k, (0, 1, 3, 4, 2))
    cvt = jnp.transpose(cache_v, (0, 1, 3, 4, 2))
    row2 = lambda g: g.reshape(1, d)

    xp = x_prompt.reshape(b * s, d)
    xs = x_sample.reshape(db * t, d)
    tm_p = _pick(s, (512, 256, 128))
    tm_s = db * t
    cp_l, ks_l, vs_l, cs_l = [], [], [], []
    kv_all = None
    trow = jnp.arange(db * t) % t
    for l in range(depth):
        win = w_in[l].astype(BF16)
        wkv_t = w_in[l][:, sbw:3 * sbw].T.astype(BF16)
        wa, wb, wo = w_a[l].astype(BF16), w_b[l].astype(BF16), w_o[l].astype(BF16)
        last = l == depth - 1
        final_g = row2(norm_final) if last else None
        fi = l // 2
        if l % 2 == 0:
            mode = "dense"
            ffn = (ffn_wg[fi].astype(BF16), ffn_wu[fi].astype(BF16), ffn_wd[fi].astype(BF16))
        else:
            mode = "moe"
            ffn = router[fi]
            ewg, ewu, ewd = moe_wg[fi].astype(BF16), moe_wu[fi].astype(BF16), moe_wd[fi].astype(BF16)

        q, kt_all, vt_all, ktb, vtb, bg, u, sga, sgb = _in_proj(
            xp, row2(norm_mix[l]), win, wkv_t, b, s, sbw, dc, scale, layer=l, depth=depth, kv_all=kv_all)
        kv_all = (kt_all, vt_all)
        o = _sb_prompt(q, ktb, vtb, sb_bias[l], hd)
        res = _mix_out(xp, o, bg, u, sga, sgb, conv_w[l], wa, wb, wo, row2(norm_ffn[l]), seq=s, hist=None,
                       mode=mode, ffn=ffn, final_g=final_g if mode == "dense" else None, tm=tm_p)
        xp = res if mode == "dense" else _moe(*res, ewg, ewu, ewd, final_g, tm_p)
        cp_l.append(u.reshape(b, s, dc)[:, s - (conv_w.shape[1] - 1):])

        q, kt, vt, _, _, bg, u, sga, sgb = _in_proj(xs, row2(norm_mix[l]), win, wkv_t, 1, db * t, sbw, dc, scale)
        kt, vt = kt[0], vt[0]
        o = _sb_sample(q.reshape(db, t, sbw), kt[0], vt[0], sb_bias[l], ckt, cvt, l, page_table, hd)
        st = state_conv[l]
        h1 = jnp.repeat(st[:, 1], t, axis=0)
        h2 = jnp.where((trow == 0)[:, None], jnp.repeat(st[:, 0], t, axis=0), h1)
        res = _mix_out(xs, o.reshape(db * t, sbw), bg, u, sga, sgb, conv_w[l], wa, wb, wo, row2(norm_ffn[l]),
                       seq=t, hist=(h1, h2), mode=mode, ffn=ffn,
                       final_g=final_g if mode == "dense" else None, tm=tm_s)
        xs = res if mode == "dense" else _moe(*res, ewg, ewu, ewd, final_g, tm_s)
        ks_l.append(kt[0].T.reshape(db, t, h, hd))
        vs_l.append(vt[0].T.reshape(db, t, h, hd))
        cs_l.append(jnp.concatenate([st, u.reshape(db, t, dc)], axis=1)[:, t:])

    seq_major = lambda a: jnp.transpose(a.reshape(depth, b, h, hd, s), (0, 1, 4, 2, 3))
    return (xp.reshape(b, s, d), xs.reshape(db, t, d), seq_major(kv_all[0]), seq_major(kv_all[1]), jnp.stack(cp_l),
            jnp.stack(ks_l), jnp.stack(vs_l), jnp.stack(cs_l))
```

```python
import functools

import jax
import jax.numpy as jnp
from jax import lax
from jax.experimental import pallas as pl
from jax.experimental.pallas import tpu as pltpu

F32 = jnp.float32
BF16 = jnp.bfloat16
I32 = jnp.int32

RMS_EPS = 1e-6
TOP_K = 2
LANES = 128
SUBLANES = 8
BF16_ROWS = 16
MXU_DIM = 256
LOG2E = 1.4426950408889634
BIAS_TERMS = 3
WIN = 256
MOE_COARSE = 1024
ROW_BITS = 10
VMEM_LIMIT = 56 * 1024 * 1024


def _pick(n, cands):
    for c in cands:
        if n % c == 0:
            return c
    return n


def _params(sem):
    return pltpu.CompilerParams(dimension_semantics=sem, vmem_limit_bytes=VMEM_LIMIT)


def _const_spec(shape):
    nd = len(shape)
    return pl.BlockSpec(shape, lambda *_: (0,) * nd, pipeline_mode=pl.Buffered(1))


def _rms(x, g):
    return x * lax.rsqrt(jnp.mean(x * x, axis=-1, keepdims=True) + RMS_EPS) * g


def _sigmoid(x):
    return 1.0 / (1.0 + jnp.exp(-x))


def _in_proj_kernel(x_ref, g_ref, w_ref, *rest, sbw, dc, d, scale):
    q_ref, kt_ref, vt_ref, ktb_ref, vtb_ref, bg_ref, u_ref, sga_ref, sgb_ref = rest[-9:]
    xn = _rms(x_ref[...], g_ref[...]).astype(BF16)

    def proj(lo, n):
        return jnp.dot(xn, w_ref[:, lo:lo + n], preferred_element_type=F32)

    def proj_t(lo):
        return lax.dot_general(w_ref[:, lo:lo + sbw], xn, (((0,), (1,)), ((), ())),
                               preferred_element_type=F32)

    q_ref[...] = (proj(0, sbw) * scale).astype(BF16)
    kt = proj_t(sbw)
    kt_ref[...] = kt
    ktb_ref[...] = kt.astype(BF16)
    vt = proj_t(2 * sbw)
    vt_ref[...] = vt
    vtb_ref[...] = vt.astype(BF16)
    o = 3 * sbw
    bg_ref[...] = proj(o, dc).astype(BF16)
    u_ref[...] = proj(o + dc, dc) * proj(o + 2 * dc, dc)
    o += 3 * dc
    sga_ref[...] = _sigmoid(proj(o, d)).astype(BF16)
    sgb_ref[...] = _sigmoid(proj(o + d, d)).astype(BF16)


def _in_proj(x, g, w, b, s, sbw, dc, scale, layer=0, depth=1, kv_all=None):
    n, d = x.shape
    tm = _pick(s, (512, 256, 128))
    nt = s // tm
    row = lambda c: pl.BlockSpec((tm, c), lambda i: (i, 0))
    col = lambda: pl.BlockSpec((None, sbw, tm), lambda i: (i // nt, 0, i % nt))
    kv = lambda: pl.BlockSpec((None, None, sbw, tm), lambda i: (layer, i // nt, 0, i % nt))
    outs = [(row(sbw), (n, sbw), BF16), (kv(), (depth, b, sbw, s), F32), (kv(), (depth, b, sbw, s), F32),
            (col(), (b, sbw, s), BF16), (col(), (b, sbw, s), BF16), (row(dc), (n, dc), BF16),
            (row(dc), (n, dc), F32), (row(d), (n, d), BF16), (row(d), (n, d), BF16)]
    ins = [x, g, w]
    specs = [row(d), _const_spec((1, d)), _const_spec(w.shape)]
    aliases = {}
    if kv_all is not None:
        aliases = {len(ins): 1, len(ins) + 1: 2}
        ins += list(kv_all)
        specs += [pl.BlockSpec(memory_space=pl.ANY)] * 2
    return pl.pallas_call(
        functools.partial(_in_proj_kernel, sbw=sbw, dc=dc, d=d, scale=scale),
        grid=(n // tm,),
        in_specs=specs,
        out_specs=[o[0] for o in outs],
        out_shape=[jax.ShapeDtypeStruct(o[1], o[2]) for o in outs],
        input_output_aliases=aliases,
        compiler_params=_params(("parallel",)),
        name="in_proj",
    )(*ins)


def _sb_scores(q, kt, bias):
    z = jnp.dot(q, kt, preferred_element_type=F32)
    if bias is not None:
        z = z + bias
    sp = jnp.maximum(z, 0.0) + jnp.log(1.0 + jnp.exp2(-jnp.abs(z))) * LOG2E
    return sp.astype(BF16), z


def _sb_suffix(sp, nt, mask):
    if mask is not None:
        sp = jnp.where(mask, sp, jnp.zeros_like(sp))
    e = jnp.dot(sp, nt, preferred_element_type=F32)
    return e, e[:, :1]


def _sb_weights(z, e, c, mask):
    aw = jnp.exp2(z + e + c)
    if mask is not None:
        aw = jnp.where(mask, aw, 0.0)
    return aw.astype(BF16)


def _sb_pv(aw, vt):
    return lax.dot_general(aw, vt, (((1,), (1,)), ((), ())), preferred_element_type=F32)


def _sb_chunk(q, kt, vt, nt, bias, c, mask):
    sub = nt.shape[0]
    sp, z = _sb_scores(q, kt, bias)
    aws = []
    for blk in reversed(range(kt.shape[1] // sub)):
        sl = slice(blk * sub, (blk + 1) * sub)
        m = None if mask is None else mask[:, sl]
        e, tot = _sb_suffix(sp[:, sl], nt, m)
        aws.append(_sb_weights(z[:, sl], e, c, m))
        c = c + tot
    aw = jnp.concatenate(aws[::-1], axis=1) if len(aws) > 1 else aws[0]
    return c, _sb_pv(aw, vt)


def _sb_prompt_kernel(bias_ref, q_ref, kt_ref, vt_ref, nt_ref, o_ref, acc_ref, *, tq, hd, hpg, ng):
    g = pl.program_id(1)
    i = pl.program_id(2)
    w = hpg * hd
    nt = nt_ref[...]
    lane = lax.broadcasted_iota(I32, (tq, w), 1)
    row = lax.broadcasted_iota(I32, (tq, tq), 0)
    col = lax.broadcasted_iota(I32, (tq, tq), 1)
    causal = col < row
    in_head = [(lane >= a * hd) & (lane < (a + 1) * hd) for a in range(hpg)]
    qs = []
    for gi in range(ng):
        q2 = q_ref[:, gi * w:(gi + 1) * w].astype(F32)
        for a in range(hpg):
            head = (g * ng + gi) * hpg + a
            ext = jnp.zeros((tq, w), F32)
            for k in range(BIAS_TERMS):
                ext = jnp.where(lane == k, bias_ref[head * BIAS_TERMS + k], ext)
            qs.append(jnp.concatenate([jnp.where(in_head[a], q2, 0.0), ext], axis=1).astype(BF16))
    ones_rows = jnp.where(lax.broadcasted_iota(I32, (w, tq), 0) < BIAS_TERMS, 1.0, 0.0).astype(BF16)

    sub = nt.shape[0]

    def chunk(j, cs, diagonal):
        keys = pl.ds(pl.multiple_of(j * tq, tq), tq)
        out = []
        for gi in range(ng):
            kt = jnp.concatenate([kt_ref[gi * w:(gi + 1) * w, keys], ones_rows], axis=0)
            vt = vt_ref[gi * w:(gi + 1) * w, keys]
            for a in range(hpg):
                k = gi * hpg + a
                if diagonal:
                    parts = []
                    for rb in range(tq // sub):
                        rows = slice(rb * sub, (rb + 1) * sub)
                        nk = (rb + 1) * sub
                        c, pv = _sb_chunk(qs[k][rows], kt[:, :nk], vt[:, :nk], nt, None,
                                          jnp.zeros((sub, 1), F32), causal[rows, :nk])
                        acc_ref[k, rows, :] = pv
                        parts.append(c)
                    out.append(jnp.concatenate(parts, axis=0))
                else:
                    c, pv = _sb_chunk(qs[k], kt, vt, nt, None, cs[k], None)
                    acc_ref[k] += pv
                    out.append(c)
        return tuple(out)

    cs = chunk(i, None, True)
    lax.fori_loop(0, i, lambda jj, cs: chunk(i - 1 - jj, cs, False), cs)
    for gi in range(ng):
        out = acc_ref[gi * hpg]
        for a in range(1, hpg):
            out = jnp.where(in_head[a], acc_ref[gi * hpg + a], out)
        o_ref[:, gi * w:(gi + 1) * w] = out.astype(BF16)


def _split_bf16(x):
    terms = []
    for _ in range(BIAS_TERMS):
        t = x.astype(BF16).astype(F32)
        terms.append(t)
        x = x - t
    return jnp.stack(terms, axis=-1)


def _neg_tri(t):
    j = lax.broadcasted_iota(I32, (t, t), 0)
    s = lax.broadcasted_iota(I32, (t, t), 1)
    return jnp.where(j >= s, -1.0, 0.0).astype(BF16)


def _sb_prompt(q, ktb, vtb, bias, hd):
    b, sbw, s = ktb.shape
    n = q.shape[0]
    hpg = max(1, LANES // hd)
    w = hpg * hd
    ng = _pick(sbw // w, (2, 1))
    gw = ng * w
    tq = _pick(s, (1024, 512, 256, 128))
    sub = _pick(tq, (MXU_DIM, LANES))
    nq = s // tq
    return pl.pallas_call(
        functools.partial(_sb_prompt_kernel, tq=tq, hd=hd, hpg=hpg, ng=ng),
        grid=(b, sbw // gw, nq),
        in_specs=[pl.BlockSpec(memory_space=pltpu.SMEM),
                  pl.BlockSpec((tq, gw), lambda bi, g, i: (bi * nq + i, g)),
                  pl.BlockSpec((None, gw, s), lambda bi, g, i: (bi, g, 0)),
                  pl.BlockSpec((None, gw, s), lambda bi, g, i: (bi, g, 0)),
                  _const_spec((sub, sub))],
        out_specs=pl.BlockSpec((tq, gw), lambda bi, g, i: (bi * nq + i, g)),
        out_shape=jax.ShapeDtypeStruct((n, sbw), BF16),
        scratch_shapes=[pltpu.VMEM((ng * hpg, tq, w), F32)],
        compiler_params=_params(("parallel", "parallel", "arbitrary")),
        name="sb_prompt",
    )(_split_bf16(bias * LOG2E).reshape(-1), q, ktb, vtb, _neg_tri(sub))


def _sb_sample_kernel(pt_ref, q_ref, bias_ref, kn_ref, vn_ref, ntn_ref, nt_ref, *rest, pg, t, h, hd):
    k_refs = rest[:pg]
    v_refs = rest[pg:2 * pg]
    o_ref = rest[2 * pg]
    c_ref, acc_ref = rest[2 * pg + 1:]
    bi = pl.program_id(0)
    s = pl.program_id(1)
    sbw = h * hd
    rows = t * h
    bias = bias_ref[...]
    q = q_ref[...].astype(F32)
    qrows = jnp.concatenate([jnp.broadcast_to(q[i:i + 1], (h, sbw)) for i in range(t)], axis=0)
    r = lax.broadcasted_iota(I32, (rows, sbw), 0)
    cidx = lax.broadcasted_iota(I32, (rows, sbw), 1)
    headmask = (cidx // hd) == (r % h)
    qbd = jnp.where(headmask, qrows, 0.0).astype(BF16)

    @pl.when(s == 0)
    def _():
        ns = kn_ref.shape[1]
        rq = lax.broadcasted_iota(I32, (rows, ns), 0) // h
        ck = lax.broadcasted_iota(I32, (rows, ns), 1)
        mask = (ck // t == bi) & (ck % t < rq)
        c, pv = _sb_chunk(qbd, kn_ref[...].astype(BF16), vn_ref[...].astype(BF16), ntn_ref[...], bias,
                          jnp.zeros((rows, 1), F32), mask)
        c_ref[...] = c
        acc_ref[...] = pv

    nt = nt_ref[...]
    page = nt.shape[0]
    scores = [_sb_scores(qbd, k_refs[i][...].reshape(sbw, page).astype(BF16), bias) for i in range(pg)]
    sums = [_sb_suffix(sp, nt, None) for sp, _ in scores]
    c = c_ref[...]
    pv = acc_ref[...]
    for i in reversed(range(pg)):
        aw = _sb_weights(scores[i][1], sums[i][0], c, None)
        pv = pv + _sb_pv(aw, v_refs[i][...].reshape(sbw, page).astype(BF16))
        c = c + sums[i][1]
    c_ref[...] = c
    acc_ref[...] = pv

    @pl.when(s == pl.num_programs(1) - 1)
    def _():
        acc = jnp.where(headmask, acc_ref[...], 0.0)
        o_ref[...] = jnp.sum(acc.reshape(t, h, sbw), axis=1).astype(BF16)


def _sb_sample(q, kt_new, vt_new, bias, cache_kt, cache_vt, layer, page_table, hd):
    db, t, sbw = q.shape
    h = sbw // hd
    page = cache_kt.shape[4]
    ns = kt_new.shape[1]
    n_pages = page_table.shape[1]
    pg = _pick(n_pages, (32, 16, 8, 4, 2, 1))
    n_steps = n_pages // pg
    rows = t * h
    bias_rows = (jnp.tile(bias, t) * LOG2E).reshape(rows, 1)

    def page_spec(i):
        def imap(bi, s, pt):
            return (layer, pt[bi * n_pages + (n_steps - 1 - s) * pg + i], 0, 0, 0)
        return pl.BlockSpec((None, None, h, hd, page), imap)

    const = lambda shape: pl.BlockSpec(shape, lambda bi, s, pt: (0,) * len(shape))
    qspec = pl.BlockSpec((None, t, sbw), lambda bi, s, pt: (bi, 0, 0))
    gs = pltpu.PrefetchScalarGridSpec(
        num_scalar_prefetch=1,
        grid=(db, n_steps),
        in_specs=[qspec, const((rows, 1)), const((sbw, ns)), const((sbw, ns)), const((ns, ns)),
                  const((page, page))] + [page_spec(i) for i in range(pg)] * 2,
        out_specs=qspec,
        scratch_shapes=[pltpu.VMEM((rows, 1), F32), pltpu.VMEM((rows, sbw), F32)],
    )
    return pl.pallas_call(
        functools.partial(_sb_sample_kernel, pg=pg, t=t, h=h, hd=hd),
        grid_spec=gs,
        out_shape=jax.ShapeDtypeStruct((db, t, sbw), BF16),
        compiler_params=_params(("parallel", "arbitrary")),
        name="sb_sample",
    )(page_table.reshape(-1), q, bias_rows, kt_new, vt_new, _neg_tri(ns), _neg_tri(page),
      *([cache_kt] * pg), *([cache_vt] * pg))


def _mix_out_kernel(*refs, seq, period, mode, final_norm, n_exp):
    it = iter(refs)
    x_ref, o_ref, bg_ref, u_ref = next(it), next(it), next(it), next(it)
    if period is None:
        uprev_ref = next(it)
    else:
        h1_ref, h2_ref = next(it), next(it)
    sga_ref, sgb_ref, cw_ref, wa_ref, wb_ref, wo_ref, gf_ref = (next(it) for _ in range(7))
    if mode == "dense":
        wg_ref, wu_ref, wd_ref = next(it), next(it), next(it)
        if final_norm:
            gn_ref = next(it)
        y_ref = next(it)
    else:
        rt_ref, tri_ref = next(it), next(it)
        x1_ref, xn_ref, ri_ref, rg_ref, cnt_ref = (next(it) for _ in range(5))

    tm = x_ref.shape[0]
    u = u_ref[...]
    row = lax.broadcasted_iota(I32, u.shape, 0)
    u1 = pltpu.roll(u, 1, axis=0)
    u2 = pltpu.roll(u, 2, axis=0)
    if period is None:
        first = (pl.program_id(0) % (seq // tm)) == 0
        keep = jnp.where(first, 0.0, 1.0)
        hm1 = uprev_ref[SUBLANES - 1:SUBLANES, :] * keep
        hm2 = uprev_ref[SUBLANES - 2:SUBLANES - 1, :] * keep
        u1 = jnp.where(row == 0, hm1, u1)
        u2 = jnp.where(row == 0, hm2, jnp.where(row == 1, hm1, u2))
    else:
        u1 = jnp.where(row % period == 0, h1_ref[...], u1)
        u2 = jnp.where(row % period < 2, h2_ref[...], u2)
    cv = cw_ref[0:1, :] * u2 + cw_ref[1:2, :] * u1 + cw_ref[2:3, :] * u

    ya = jnp.dot(o_ref[...], wa_ref[...], preferred_element_type=F32)
    yb = jnp.dot((bg_ref[...].astype(F32) * cv).astype(BF16), wb_ref[...], preferred_element_type=F32)
    mix = (sga_ref[...].astype(F32) * ya + sgb_ref[...].astype(F32) * yb).astype(BF16)
    x1 = x_ref[...] + jnp.dot(mix, wo_ref[...], preferred_element_type=F32)
    xnf = _rms(x1, gf_ref[...])
    xn = xnf.astype(BF16)

    if mode == "dense":
        hg = jnp.dot(xn, wg_ref[...], preferred_element_type=F32)
        hu = jnp.dot(xn, wu_ref[...], preferred_element_type=F32)
        hh = (hg * _sigmoid(hg) * hu).astype(BF16)
        y = x1 + jnp.dot(hh, wd_ref[...], preferred_element_type=F32)
        if final_norm:
            y = _rms(y, gn_ref[...])
        y_ref[...] = y
        return

    x1_ref[...] = x1
    xn_ref[...] = xn
    ep = rt_ref.shape[0]
    rt = rt_ref[...]
    rh = rt.astype(BF16)
    rl = (rt - rh.astype(F32)).astype(BF16)
    xl = (xnf - xn.astype(F32)).astype(BF16)
    dn = (((1,), (1,)), ((), ()))
    logits = (lax.dot_general(rh, xn, dn, preferred_element_type=F32)
              + lax.dot_general(rl, xn, dn, preferred_element_type=F32)
              + lax.dot_general(rh, xl, dn, preferred_element_type=F32))
    ie = lax.broadcasted_iota(I32, (ep, tm), 0)
    logits = jnp.where(ie < n_exp, logits, -jnp.inf)
    m1 = jnp.max(logits, axis=0, keepdims=True)
    i1 = jnp.min(jnp.where(logits == m1, ie, ep), axis=0, keepdims=True)
    l2 = jnp.where(ie == i1, -jnp.inf, logits)
    m2 = jnp.max(l2, axis=0, keepdims=True)
    i2 = jnp.min(jnp.where(l2 == m2, ie, ep), axis=0, keepdims=True)
    e2 = jnp.exp(m2 - m1)
    g1 = 1.0 / (1.0 + e2)
    g2 = e2 / (1.0 + e2)
    oh = jnp.where((ie == i1) | (ie == i2), 1.0, 0.0)
    before = jnp.dot(oh.astype(BF16), tri_ref[...], preferred_element_type=F32)
    w1 = jnp.sum(jnp.where(ie == i1, before, 0.0), axis=0, keepdims=True).astype(I32)
    w2 = jnp.sum(jnp.where(ie == i2, before, 0.0), axis=0, keepdims=True).astype(I32)
    r8 = lax.broadcasted_iota(I32, (SUBLANES, tm), 0)
    ri_ref[...] = jnp.where(r8 == 0, i1, jnp.where(r8 == 1, i2, jnp.where(r8 == 2, w1, jnp.where(r8 == 3, w2, 0))))
    rg_ref[...] = jnp.where(r8 == 0, g1, jnp.where(r8 == 1, g2, 0.0))
    cnt_ref[...] = jnp.broadcast_to(jnp.sum(oh, axis=1, keepdims=True), (ep, LANES))


def _mix_out(x, o, bg, u, sga, sgb, cw, wa, wb, wo, gf, *, seq, hist, mode, ffn, final_g, tm):
    n, d = x.shape
    sbw, dc = o.shape[1], bg.shape[1]
    nt = n // tm
    row = lambda c: pl.BlockSpec((tm, c), lambda i: (i, 0))
    ins = [x, o, bg, u]
    specs = [row(d), row(sbw), row(dc), row(dc)]
    if hist is None:
        assert seq % tm == 0 and tm % SUBLANES == 0
        period = None
        ins.append(u)
        specs.append(pl.BlockSpec((SUBLANES, dc), lambda i: (jnp.maximum(i * (tm // SUBLANES) - 1, 0), 0)))
    else:
        assert tm % seq == 0
        period = seq
        ins += list(hist)
        specs += [row(dc), row(dc)]
    cwp = jnp.pad(cw, ((0, SUBLANES - cw.shape[0]), (0, 0)))
    ins += [sga, sgb, cwp, wa, wb, wo, gf]
    specs += [row(d), row(d)] + [_const_spec(a.shape) for a in (cwp, wa, wb, wo, gf)]
    if mode == "dense":
        ins += list(ffn)
        specs += [_const_spec(a.shape) for a in ffn]
        if final_g is not None:
            ins.append(final_g)
            specs.append(_const_spec(final_g.shape))
        out_specs = row(d)
        out_shape = jax.ShapeDtypeStruct((n, d), F32)
        n_exp = 0
    else:
        router = ffn
        n_exp = router.shape[1]
        ep = -(-n_exp // BF16_ROWS) * BF16_ROWS
        rt = jnp.pad(router.T, ((0, ep - n_exp), (0, 0)))
        ti = lax.broadcasted_iota(I32, (tm, tm), 0)
        tj = lax.broadcasted_iota(I32, (tm, tm), 1)
        tri = jnp.where(ti < tj, 1.0, 0.0).astype(BF16)
        ins += [rt, tri]
        specs += [_const_spec(rt.shape), _const_spec(tri.shape)]
        lane_rows = lambda: pl.BlockSpec((SUBLANES, tm), lambda i: (0, i))
        out_specs = [row(d), row(d), lane_rows(), lane_rows(), pl.BlockSpec((None, ep, LANES), lambda i: (i, 0, 0))]
        out_shape = [jax.ShapeDtypeStruct((n, d), F32), jax.ShapeDtypeStruct((n, d), BF16),
                     jax.ShapeDtypeStruct((SUBLANES, n), I32), jax.ShapeDtypeStruct((SUBLANES, n), F32),
                     jax.ShapeDtypeStruct((nt, ep, LANES), F32)]
    return pl.pallas_call(
        functools.partial(_mix_out_kernel, seq=seq, period=period, mode=mode,
                          final_norm=final_g is not None, n_exp=n_exp),
        grid=(nt,),
        in_specs=specs,
        out_specs=out_specs,
        out_shape=out_shape,
        compiler_params=_params(("parallel",)),
        name="mix_out_" + mode,
    )(*ins)


def _windowed(body, r0, r1, size, win):
    half = size // 2
    if half % win or half == 0:
        body(0, size)
        return
    start = jnp.minimum(r0 & (-win), size - half)
    fits = r1 <= start + half

    @pl.when(fits)
    def _():
        body(pl.multiple_of(start, win), half)

    @pl.when(jnp.logical_not(fits))
    def _():
        body(0, size)


def _dispatch_kernel(ii_ref, ij_ref, fl_ref, r0_ref, r1_ref, n_ref, x_ref, pos_ref, rg_ref, xs_ref, gs_ref,
                     *, ts, win):
    w = pl.program_id(0)

    @pl.when(w < n_ref[0])
    def _():
        tt = x_ref.shape[0]

        @pl.when(fl_ref[w] == 1)
        def _():
            xs_ref[...] = jnp.zeros_like(xs_ref)
            gs_ref[...] = jnp.zeros_like(gs_ref)

        def fill(start, size):
            rows = pl.ds(start, size)
            r = ii_ref[w] * ts + start + lax.broadcasted_iota(I32, (size, tt), 0)
            m1 = r == pos_ref[0:1, :]
            m2 = r == pos_ref[1:2, :]
            sel = jnp.where(m1 | m2, 1.0, 0.0).astype(BF16)
            xs = jnp.dot(sel, x_ref[...], preferred_element_type=F32).astype(BF16)
            gate = jnp.sum(jnp.where(m1, rg_ref[0:1, :], 0.0) + jnp.where(m2, rg_ref[1:2, :], 0.0),
                           axis=1, keepdims=True)
            xs_ref[rows, :] += xs
            gs_ref[rows, :] += jnp.broadcast_to(gate, (size, gs_ref.shape[1]))

        _windowed(fill, r0_ref[w], r1_ref[w], ts, win)


def _dispatch(xn, pos, rg, items, n_items, ts, tt, rows):
    n, d = xn.shape
    ii, ij, first, r0, r1 = items
    win = _pick(ts, (WIN,))
    tok = lambda w, ii, ij, *_: (ij[w], 0)
    tok_t = lambda w, ii, ij, *_: (0, ij[w])
    srt = lambda w, ii, *_: (ii[w], 0)
    gs = pltpu.PrefetchScalarGridSpec(
        num_scalar_prefetch=6,
        grid=(ii.shape[0],),
        in_specs=[pl.BlockSpec((tt, d), tok), pl.BlockSpec((SUBLANES, tt), tok_t),
                  pl.BlockSpec((SUBLANES, tt), tok_t)],
        out_specs=[pl.BlockSpec((ts, d), srt), pl.BlockSpec((ts, LANES), srt)],
    )
    return pl.pallas_call(
        functools.partial(_dispatch_kernel, ts=ts, win=win),
        grid_spec=gs,
        out_shape=[jax.ShapeDtypeStruct((rows, d), BF16), jax.ShapeDtypeStruct((rows, LANES), F32)],
        compiler_params=_params(("arbitrary",)),
        name="moe_dispatch",
    )(ii, ij, first, r0, r1, n_items, xn, pos, rg)


def _experts_kernel(te_ref, nv_ref, xs_ref, wg_ref, wu_ref, wd_ref, gs_ref, y_ref, acc_ref):
    i = pl.program_id(0)
    c = pl.program_id(1)

    @pl.when(i < nv_ref[0])
    def _():
        x = xs_ref[...]
        hg = jnp.dot(x, wg_ref[...], preferred_element_type=F32)
        hu = jnp.dot(x, wu_ref[...], preferred_element_type=F32)
        hh = (hg * _sigmoid(hg) * hu).astype(BF16)
        part = jnp.dot(hh, wd_ref[...], preferred_element_type=F32)

        @pl.when(c == 0)
        def _():
            acc_ref[...] = part

        @pl.when(c > 0)
        def _():
            acc_ref[...] += part

        @pl.when(c == pl.num_programs(1) - 1)
        def _():
            y_ref[...] = (acc_ref[...] * gs_ref[:, 0:1]).astype(BF16)


def _experts(xs, gsort, wg, wu, wd, tile_expert, n_valid, ts):
    rows, d = xs.shape
    dff = wg.shape[2]
    fc = _pick(dff, (2048, 1792, 1536, 1280, 1024, 896, 768, 640, 512, 384, 256, 128))
    nfc = dff // fc
    tile = lambda i, nv: jnp.maximum(jnp.minimum(i, nv[0] - 1), 0)
    chunk = lambda i, c, nv: jnp.where(i < nv[0], c, nfc - 1)
    gs = pltpu.PrefetchScalarGridSpec(
        num_scalar_prefetch=2,
        grid=(rows // ts, nfc),
        in_specs=[pl.BlockSpec((ts, d), lambda i, c, te, nv: (tile(i, nv), 0)),
                  pl.BlockSpec((None, d, fc), lambda i, c, te, nv: (te[i], 0, chunk(i, c, nv))),
                  pl.BlockSpec((None, d, fc), lambda i, c, te, nv: (te[i], 0, chunk(i, c, nv))),
                  pl.BlockSpec((None, fc, d), lambda i, c, te, nv: (te[i], chunk(i, c, nv), 0)),
                  pl.BlockSpec((ts, LANES), lambda i, c, te, nv: (tile(i, nv), 0))],
        out_specs=pl.BlockSpec((ts, d), lambda i, c, te, nv: (tile(i, nv), 0)),
        scratch_shapes=[pltpu.VMEM((ts, d), F32)],
    )
    return pl.pallas_call(
        _experts_kernel,
        grid_spec=gs,
        out_shape=jax.ShapeDtypeStruct((rows, d), BF16),
        compiler_params=_params(("arbitrary", "arbitrary")),
        name="moe_experts",
    )(tile_expert, n_valid, xs, wg, wu, wd, gsort)


def _combine_kernel(ii_ref, ij_ref, fl_ref, r0_ref, r1_ref, n_ref, x1_ref, ys_ref, posc_ref, gn_ref, y_ref,
                    *, ts, win, final_norm):
    w = pl.program_id(0)

    @pl.when(w < n_ref[0])
    def _():
        tt = x1_ref.shape[0]
        fl = fl_ref[w]

        @pl.when(fl % 2 == 1)
        def _():
            y_ref[...] = x1_ref[...]

        def gather(start, size):
            r = ii_ref[w] * ts + start + lax.broadcasted_iota(I32, (tt, size), 1)
            sel = jnp.where((r == posc_ref[:, 0:1]) | (r == posc_ref[:, 1:2]), 1.0, 0.0).astype(BF16)
            y_ref[...] += jnp.dot(sel, ys_ref[pl.ds(start, size), :], preferred_element_type=F32)

        _windowed(gather, r0_ref[w], r1_ref[w], ts, win)

        if final_norm:
            @pl.when(fl >= 2)
            def _():
                y_ref[...] = _rms(y_ref[...], gn_ref[...])


def _combine(x1, ys, posc, items, n_items, final_g, ts, tt):
    n, d = x1.shape
    ii, ij, flags, r0, r1 = items
    win = _pick(ts, (MXU_DIM,))
    gn = final_g if final_g is not None else jnp.ones((1, d), F32)
    tok = lambda w, ii, ij, *_: (ij[w], 0)
    gs = pltpu.PrefetchScalarGridSpec(
        num_scalar_prefetch=6,
        grid=(ii.shape[0],),
        in_specs=[pl.BlockSpec((tt, d), tok),
                  pl.BlockSpec((ts, d), lambda w, ii, *_: (ii[w], 0)),
                  pl.BlockSpec((tt, LANES), tok),
                  pl.BlockSpec((1, d), lambda w, *_: (0, 0))],
        out_specs=pl.BlockSpec((tt, d), tok),
    )
    return pl.pallas_call(
        functools.partial(_combine_kernel, ts=ts, win=win, final_norm=final_g is not None),
        grid_spec=gs,
        out_shape=jax.ShapeDtypeStruct((n, d), F32),
        compiler_params=_params(("arbitrary",)),
        name="moe_combine",
    )(ii, ij, flags, r0, r1, n_items, x1, ys, posc, gn)


def _route_tables(ri, cnt, n_exp, tt, ts, g):
    n = ri.shape[1]
    nj = n // tt
    cnt = cnt[:, :n_exp, 0].astype(I32)
    tot = jnp.sum(cnt, axis=0)
    gsize = (tot + ts - 1) // ts * ts
    gend = jnp.cumsum(gsize)
    goff = gend - gsize
    seg_start = goff[None, :] + jnp.cumsum(cnt, axis=0) - cnt
    e1, e2, w1, w2 = ri[0], ri[1], ri[2], ri[3]
    oh = lambda e: e[:, None] == jnp.arange(n_exp, dtype=I32)[None, :]
    base = jnp.repeat(seg_start, tt, axis=0)
    pos1 = jnp.sum(jnp.where(oh(e1), base, 0), axis=1) + w1
    pos2 = jnp.sum(jnp.where(oh(e2), base, 0), axis=1) + w2
    two = lambda k, a, b: jnp.where(k == 0, a, jnp.where(k == 1, b, 0))
    pos = two(jnp.arange(SUBLANES, dtype=I32)[:, None], pos1[None, :], pos2[None, :])
    posc = two(jnp.arange(LANES, dtype=I32)[None, :], pos1[:, None], pos2[:, None])

    rows = -(-(n * TOP_K + n_exp * ts) // g) * g
    gm = g // tt
    njd, nid = nj // gm, rows // g
    s_lo = seg_start.reshape(njd, gm, n_exp)[:, 0]
    s_hi = s_lo + jnp.sum(cnt.reshape(njd, gm, n_exp), axis=1)
    t0 = (jnp.arange(nid, dtype=I32) * g)[:, None, None]
    lo = jnp.maximum(s_lo[None], t0)
    hi = jnp.minimum(s_hi[None], t0 + g)
    hit = hi > lo
    it_v = jnp.any(hit, axis=2)
    it_r0 = jnp.min(jnp.where(hit, lo - t0, g), axis=2)
    it_r1 = jnp.max(jnp.where(hit, hi - t0, 0), axis=2)
    n_items = jnp.sum(it_v).astype(I32).reshape(1)
    w_max = min(nid * njd, n_exp * njd + nid)
    assert nid * njd < 2 ** (31 - 2 * ROW_BITS) and g <= 2 ** ROW_BITS

    def ordered(major, minor, n_minor):
        code = ((major * n_minor + minor) << (2 * ROW_BITS)) | (it_r0 << ROW_BITS) | (it_r1 - 1)
        code = jnp.sort(jnp.where(it_v, code, jnp.iinfo(jnp.int32).max).reshape(-1))[:w_max]
        live = jnp.arange(w_max) < n_items[0]
        code = jnp.where(live, code, code[jnp.maximum(n_items[0] - 1, 0)])
        pair = code >> (2 * ROW_BITS)
        mask = (1 << ROW_BITS) - 1
        return pair // n_minor, pair % n_minor, live, (code >> ROW_BITS) & mask, (code & mask) + 1

    ii = jnp.arange(nid, dtype=I32)[:, None]
    jj = jnp.arange(njd, dtype=I32)[None, :]
    si, sj, live, r0, r1 = ordered(ii, jj, njd)
    first = jnp.concatenate([jnp.ones((1,), bool), si[1:] != si[:-1]]) & live
    disp = (si, sj, first.astype(I32), r0, r1)
    cj, ci, live, r0, r1 = ordered(jj, ii, nid)
    cfirst = jnp.concatenate([jnp.ones((1,), bool), cj[1:] != cj[:-1]]) & live
    nxt_live = jnp.concatenate([live[1:], jnp.zeros((1,), bool)])
    clast = (jnp.concatenate([cj[1:] != cj[:-1], jnp.ones((1,), bool)]) | ~nxt_live) & live
    comb = (ci, cj, cfirst.astype(I32) + 2 * clast.astype(I32), r0, r1)

    ni = rows // ts
    n_valid = (gend[-1] // ts).astype(I32).reshape(1)
    tstart = jnp.arange(ni, dtype=I32) * ts
    te = jnp.sum(tstart[:, None] >= gend[None, :], axis=1).astype(I32)
    te_last = jnp.sum((n_valid[0] - 1) * ts >= gend).astype(I32)
    te = jnp.where(jnp.arange(ni) < n_valid[0], jnp.minimum(te, n_exp - 1), te_last)
    return pos, posc, disp, comb, n_items, te, n_valid, rows


def _moe(x1, xn, ri, rg, cnt, wg, wu, wd, final_g, tt):
    n_exp = wg.shape[0]
    n = x1.shape[0]
    ts = tt
    g = tt * 2 if (n // tt) % 2 == 0 and tt * 2 <= MOE_COARSE else tt
    pos, posc, disp, comb, n_items, te, n_valid, rows = _route_tables(ri, cnt, n_exp, tt, ts, g)
    xs, gsort = _dispatch(xn, pos, rg, disp, n_items, g, g, rows)
    ys = _experts(xs, gsort, wg, wu, wd, te, n_valid, ts)
    return _combine(x1, ys, posc, comb, n_items, final_g, g, g)


def kernel(x_prompt, x_sample, cache_k, cache_v, state_conv, page_table, norm_mix, w_in, sb_bias, w_a,
           conv_w, w_b, w_o, norm_ffn, ffn_wg, ffn_wu, ffn_wd, router, moe_wg, moe_wu, moe_wd, norm_final):
    b, s, d = x_prompt.shape
    db, t, _ = x_sample.shape
    depth = w_in.shape[0]
    sbw = w_a.shape[1]
    dc = w_b.shape[1]
    h = sb_bias.shape[1]
    hd = sbw // h
    scale = hd ** -0.5 * LOG2E
    ckt = jnp.transpose(cache_k, (0, 1, 3, 4, 2))
    cvt = jnp.transpose(cache_v, (0, 1, 3, 4, 2))
    row2 = lambda g: g.reshape(1, d)

    xp = x_prompt.reshape(b * s, d)
    xs = x_sample.reshape(db * t, d)
    tm_p = _pick(s, (512, 256, 128))
    tm_s = db * t
    cp_l, ks_l, vs_l, cs_l = [], [], [], []
    kv_all = None
    trow = jnp.arange(db * t) % t
    for l in range(depth):
        win = w_in[l].astype(BF16)
        wa, wb, wo = w_a[l].astype(BF16), w_b[l].astype(BF16), w_o[l].astype(BF16)
        last = l == depth - 1
        final_g = row2(norm_final) if last else None
        fi = l // 2
        if l % 2 == 0:
            mode = "dense"
            ffn = (ffn_wg[fi].astype(BF16), ffn_wu[fi].astype(BF16), ffn_wd[fi].astype(BF16))
        else:
            mode = "moe"
            ffn = router[fi]
            ewg, ewu, ewd = moe_wg[fi].astype(BF16), moe_wu[fi].astype(BF16), moe_wd[fi].astype(BF16)

        q, kt_all, vt_all, ktb, vtb, bg, u, sga, sgb = _in_proj(
            xp, row2(norm_mix[l]), win, b, s, sbw, dc, scale, layer=l, depth=depth, kv_all=kv_all)
        kv_all = (kt_all, vt_all)
        o = _sb_prompt(q, ktb, vtb, sb_bias[l], hd)
        res = _mix_out(xp, o, bg, u, sga, sgb, conv_w[l], wa, wb, wo, row2(norm_ffn[l]), seq=s, hist=None,
                       mode=mode, ffn=ffn, final_g=final_g if mode == "dense" else None, tm=tm_p)
        xp = res if mode == "dense" else _moe(*res, ewg, ewu, ewd, final_g, tm_p)
        cp_l.append(u.reshape(b, s, dc)[:, s - (conv_w.shape[1] - 1):])

        q, kt, vt, _, _, bg, u, sga, sgb = _in_proj(xs, row2(norm_mix[l]), win, 1, db * t, sbw, dc, scale)
        kt, vt = kt[0], vt[0]
        o = _sb_sample(q.reshape(db, t, sbw), kt[0], vt[0], sb_bias[l], ckt, cvt, l, page_table, hd)
        st = state_conv[l]
        h1 = jnp.repeat(st[:, 1], t, axis=0)
        h2 = jnp.where((trow == 0)[:, None], jnp.repeat(st[:, 0], t, axis=0), h1)
        res = _mix_out(xs, o.reshape(db * t, sbw), bg, u, sga, sgb, conv_w[l], wa, wb, wo, row2(norm_ffn[l]),
                       seq=t, hist=(h1, h2), mode=mode, ffn=ffn,
                       final_g=final_g if mode == "dense" else None, tm=tm_s)
        xs = res if mode == "dense" else _moe(*res, ewg, ewu, ewd, final_g, tm_s)
        ks_l.append(kt[0].T.reshape(db, t, h, hd))
        vs_l.append(vt[0].T.reshape(db, t, h, hd))
        cs_l.append(jnp.concatenate([st, u.reshape(db, t, dc)], axis=1)[:, t:])

    seq_major = lambda a: jnp.transpose(a.reshape(depth, b, h, hd, s), (0, 1, 4, 2, 3))
    return (xp.reshape(b, s, d), xs.reshape(db, t, d), seq_major(kv_all[0]), seq_major(kv_all[1]), jnp.stack(cp_l),
            jnp.stack(ks_l), jnp.stack(vs_l), jnp.stack(cs_l))
```

```python
import functools

import jax
import jax.numpy as jnp
from jax import lax
from jax.experimental import pallas as pl
from jax.experimental.pallas import tpu as pltpu

F32 = jnp.float32
BF16 = jnp.bfloat16
I32 = jnp.int32

RMS_EPS = 1e-6
TOP_K = 2
LANES = 128
SUBLANES = 8
BF16_ROWS = 16
MXU_DIM = 256
LOG2E = 1.4426950408889634
BIAS_TERMS = 3
WIN = 256
MOE_COARSE = 1024
CAST_COLS = 256
ROW_BITS = 10
VMEM_LIMIT = 56 * 1024 * 1024


def _pick(n, cands):
    for c in cands:
        if n % c == 0:
            return c
    return n


def _params(sem):
    return pltpu.CompilerParams(dimension_semantics=sem, vmem_limit_bytes=VMEM_LIMIT)


def _const_spec(shape):
    nd = len(shape)
    return pl.BlockSpec(shape, lambda *_: (0,) * nd, pipeline_mode=pl.Buffered(1))


def _rms(x, g):
    return x * lax.rsqrt(jnp.mean(x * x, axis=-1, keepdims=True) + RMS_EPS) * g


def _sigmoid(x):
    return 1.0 / (1.0 + jnp.exp(-x))


def _in_proj_kernel(x_ref, g_ref, w_ref, *rest, sbw, dc, d, scale):
    q_ref, kt_ref, vt_ref, ktb_ref, vtb_ref, bg_ref, u_ref, sga_ref, sgb_ref = rest[-9:]
    xn = _rms(x_ref[...], g_ref[...]).astype(BF16)

    def proj(lo, n):
        return jnp.dot(xn, w_ref[:, lo:lo + n], preferred_element_type=F32)

    def proj_t(lo):
        return lax.dot_general(w_ref[:, lo:lo + sbw], xn, (((0,), (1,)), ((), ())),
                               preferred_element_type=F32)

    q_ref[...] = (proj(0, sbw) * scale).astype(BF16)
    kt = proj_t(sbw)
    kt_ref[...] = kt
    ktb_ref[...] = kt.astype(BF16)
    vt = proj_t(2 * sbw)
    vt_ref[...] = vt
    vtb_ref[...] = vt.astype(BF16)
    o = 3 * sbw
    bg_ref[...] = proj(o, dc).astype(BF16)
    u_ref[...] = proj(o + dc, dc) * proj(o + 2 * dc, dc)
    o += 3 * dc
    sga_ref[...] = _sigmoid(proj(o, d)).astype(BF16)
    sgb_ref[...] = _sigmoid(proj(o + d, d)).astype(BF16)


def _in_proj(x, g, w, b, s, sbw, dc, scale, layer=0, depth=1, kv_all=None):
    n, d = x.shape
    tm = _pick(s, (512, 256, 128))
    nt = s // tm
    row = lambda c: pl.BlockSpec((tm, c), lambda i: (i, 0))
    col = lambda: pl.BlockSpec((None, sbw, tm), lambda i: (i // nt, 0, i % nt))
    kv = lambda: pl.BlockSpec((None, None, sbw, tm), lambda i: (layer, i // nt, 0, i % nt))
    outs = [(row(sbw), (n, sbw), BF16), (kv(), (depth, b, sbw, s), F32), (kv(), (depth, b, sbw, s), F32),
            (col(), (b, sbw, s), BF16), (col(), (b, sbw, s), BF16), (row(dc), (n, dc), BF16),
            (row(dc), (n, dc), F32), (row(d), (n, d), BF16), (row(d), (n, d), BF16)]
    ins = [x, g, w]
    specs = [row(d), _const_spec((1, d)), _const_spec(w.shape)]
    aliases = {}
    if kv_all is not None:
        aliases = {len(ins): 1, len(ins) + 1: 2}
        ins += list(kv_all)
        specs += [pl.BlockSpec(memory_space=pl.ANY)] * 2
    return pl.pallas_call(
        functools.partial(_in_proj_kernel, sbw=sbw, dc=dc, d=d, scale=scale),
        grid=(n // tm,),
        in_specs=specs,
        out_specs=[o[0] for o in outs],
        out_shape=[jax.ShapeDtypeStruct(o[1], o[2]) for o in outs],
        input_output_aliases=aliases,
        compiler_params=_params(("parallel",)),
        name="in_proj",
    )(*ins)


def _sb_scores(q, kt, bias):
    z = jnp.dot(q, kt, preferred_element_type=F32)
    if bias is not None:
        z = z + bias
    sp = jnp.maximum(z, 0.0) + jnp.log(1.0 + jnp.exp2(-jnp.abs(z))) * LOG2E
    return sp.astype(BF16), z


def _sb_suffix(sp, nt, mask):
    if mask is not None:
        sp = jnp.where(mask, sp, jnp.zeros_like(sp))
    e = jnp.dot(sp, nt, preferred_element_type=F32)
    return e, e[:, :1]


def _sb_weights(z, e, c, mask):
    aw = jnp.exp2(z + e + c)
    if mask is not None:
        aw = jnp.where(mask, aw, 0.0)
    return aw.astype(BF16)


def _sb_pv(aw, vt):
    return lax.dot_general(aw, vt, (((1,), (1,)), ((), ())), preferred_element_type=F32)


def _sb_chunk(q, kt, vt, nt, bias, c, mask):
    sub = nt.shape[0]
    sp, z = _sb_scores(q, kt, bias)
    aws = []
    for blk in reversed(range(kt.shape[1] // sub)):
        sl = slice(blk * sub, (blk + 1) * sub)
        m = None if mask is None else mask[:, sl]
        e, tot = _sb_suffix(sp[:, sl], nt, m)
        aws.append(_sb_weights(z[:, sl], e, c, m))
        c = c + tot
    aw = jnp.concatenate(aws[::-1], axis=1) if len(aws) > 1 else aws[0]
    return c, _sb_pv(aw, vt)


def _sb_prompt_kernel(bias_ref, q_ref, kt_ref, vt_ref, nt_ref, o_ref, acc_ref, *, tq, hd, hpg, ng):
    g = pl.program_id(1)
    i = pl.program_id(2)
    w = hpg * hd
    nt = nt_ref[...]
    lane = lax.broadcasted_iota(I32, (tq, w), 1)
    row = lax.broadcasted_iota(I32, (tq, tq), 0)
    col = lax.broadcasted_iota(I32, (tq, tq), 1)
    causal = col < row
    in_head = [(lane >= a * hd) & (lane < (a + 1) * hd) for a in range(hpg)]
    qs = []
    for gi in range(ng):
        q2 = q_ref[:, gi * w:(gi + 1) * w].astype(F32)
        for a in range(hpg):
            head = (g * ng + gi) * hpg + a
            ext = jnp.zeros((tq, w), F32)
            for k in range(BIAS_TERMS):
                ext = jnp.where(lane == k, bias_ref[head * BIAS_TERMS + k], ext)
            qs.append(jnp.concatenate([jnp.where(in_head[a], q2, 0.0), ext], axis=1).astype(BF16))
    ones_rows = jnp.where(lax.broadcasted_iota(I32, (w, tq), 0) < BIAS_TERMS, 1.0, 0.0).astype(BF16)

    sub = nt.shape[0]

    def chunk(j, cs, diagonal):
        keys = pl.ds(pl.multiple_of(j * tq, tq), tq)
        out = []
        for gi in range(ng):
            kt = jnp.concatenate([kt_ref[gi * w:(gi + 1) * w, keys], ones_rows], axis=0)
            vt = vt_ref[gi * w:(gi + 1) * w, keys]
            for a in range(hpg):
                k = gi * hpg + a
                if diagonal:
                    parts = []
                    for rb in range(tq // sub):
                        rows = slice(rb * sub, (rb + 1) * sub)
                        nk = (rb + 1) * sub
                        c, pv = _sb_chunk(qs[k][rows], kt[:, :nk], vt[:, :nk], nt, None,
                                          jnp.zeros((sub, 1), F32), causal[rows, :nk])
                        acc_ref[k, rows, :] = pv
                        parts.append(c)
                    out.append(jnp.concatenate(parts, axis=0))
                else:
                    c, pv = _sb_chunk(qs[k], kt, vt, nt, None, cs[k], None)
                    acc_ref[k] += pv
                    out.append(c)
        return tuple(out)

    cs = chunk(i, None, True)
    lax.fori_loop(0, i, lambda jj, cs: chunk(i - 1 - jj, cs, False), cs)
    for gi in range(ng):
        out = acc_ref[gi * hpg]
        for a in range(1, hpg):
            out = jnp.where(in_head[a], acc_ref[gi * hpg + a], out)
        o_ref[:, gi * w:(gi + 1) * w] = out.astype(BF16)


def _split_bf16(x):
    terms = []
    for _ in range(BIAS_TERMS):
        t = x.astype(BF16).astype(F32)
        terms.append(t)
        x = x - t
    return jnp.stack(terms, axis=-1)


def _neg_tri(t):
    j = lax.broadcasted_iota(I32, (t, t), 0)
    s = lax.broadcasted_iota(I32, (t, t), 1)
    return jnp.where(j >= s, -1.0, 0.0).astype(BF16)


def _sb_prompt(q, ktb, vtb, bias, hd):
    b, sbw, s = ktb.shape
    n = q.shape[0]
    hpg = max(1, LANES // hd)
    w = hpg * hd
    ng = _pick(sbw // w, (2, 1))
    gw = ng * w
    tq = _pick(s, (1024, 512, 256, 128))
    sub = _pick(tq, (MXU_DIM, LANES))
    nq = s // tq
    return pl.pallas_call(
        functools.partial(_sb_prompt_kernel, tq=tq, hd=hd, hpg=hpg, ng=ng),
        grid=(b, sbw // gw, nq),
        in_specs=[pl.BlockSpec(memory_space=pltpu.SMEM),
                  pl.BlockSpec((tq, gw), lambda bi, g, i: (bi * nq + i, g)),
                  pl.BlockSpec((None, gw, s), lambda bi, g, i: (bi, g, 0)),
                  pl.BlockSpec((None, gw, s), lambda bi, g, i: (bi, g, 0)),
                  _const_spec((sub, sub))],
        out_specs=pl.BlockSpec((tq, gw), lambda bi, g, i: (bi * nq + i, g)),
        out_shape=jax.ShapeDtypeStruct((n, sbw), BF16),
        scratch_shapes=[pltpu.VMEM((ng * hpg, tq, w), F32)],
        compiler_params=_params(("parallel", "parallel", "arbitrary")),
        name="sb_prompt",
    )(_split_bf16(bias * LOG2E).reshape(-1), q, ktb, vtb, _neg_tri(sub))


def _sb_sample_kernel(pt_ref, q_ref, bias_ref, kn_ref, vn_ref, ntn_ref, nt_ref, *rest, pg, t, h, hd):
    k_refs = rest[:pg]
    v_refs = rest[pg:2 * pg]
    o_ref = rest[2 * pg]
    c_ref, acc_ref = rest[2 * pg + 1:]
    bi = pl.program_id(0)
    s = pl.program_id(1)
    sbw = h * hd
    rows = t * h
    bias = bias_ref[...]
    q = q_ref[...].astype(F32)
    qrows = jnp.concatenate([jnp.broadcast_to(q[i:i + 1], (h, sbw)) for i in range(t)], axis=0)
    r = lax.broadcasted_iota(I32, (rows, sbw), 0)
    cidx = lax.broadcasted_iota(I32, (rows, sbw), 1)
    headmask = (cidx // hd) == (r % h)
    qbd = jnp.where(headmask, qrows, 0.0).astype(BF16)

    @pl.when(s == 0)
    def _():
        ns = kn_ref.shape[1]
        rq = lax.broadcasted_iota(I32, (rows, ns), 0) // h
        ck = lax.broadcasted_iota(I32, (rows, ns), 1)
        mask = (ck // t == bi) & (ck % t < rq)
        c, pv = _sb_chunk(qbd, kn_ref[...].astype(BF16), vn_ref[...].astype(BF16), ntn_ref[...], bias,
                          jnp.zeros((rows, 1), F32), mask)
        c_ref[...] = c
        acc_ref[...] = pv

    nt = nt_ref[...]
    page = nt.shape[0]
    scores = [_sb_scores(qbd, k_refs[i][...].reshape(sbw, page).astype(BF16), bias) for i in range(pg)]
    sums = [_sb_suffix(sp, nt, None) for sp, _ in scores]
    c = c_ref[...]
    pv = acc_ref[...]
    for i in reversed(range(pg)):
        aw = _sb_weights(scores[i][1], sums[i][0], c, None)
        pv = pv + _sb_pv(aw, v_refs[i][...].reshape(sbw, page).astype(BF16))
        c = c + sums[i][1]
    c_ref[...] = c
    acc_ref[...] = pv

    @pl.when(s == pl.num_programs(1) - 1)
    def _():
        acc = jnp.where(headmask, acc_ref[...], 0.0)
        o_ref[...] = jnp.sum(acc.reshape(t, h, sbw), axis=1).astype(BF16)


def _sb_sample(q, kt_new, vt_new, bias, cache_kt, cache_vt, layer, page_table, hd):
    db, t, sbw = q.shape
    h = sbw // hd
    page = cache_kt.shape[4]
    ns = kt_new.shape[1]
    n_pages = page_table.shape[1]
    pg = _pick(n_pages, (32, 16, 8, 4, 2, 1))
    n_steps = n_pages // pg
    rows = t * h
    bias_rows = (jnp.tile(bias, t) * LOG2E).reshape(rows, 1)

    def page_spec(i):
        def imap(bi, s, pt):
            return (layer, pt[bi * n_pages + (n_steps - 1 - s) * pg + i], 0, 0, 0)
        return pl.BlockSpec((None, None, h, hd, page), imap)

    const = lambda shape: pl.BlockSpec(shape, lambda bi, s, pt: (0,) * len(shape))
    qspec = pl.BlockSpec((None, t, sbw), lambda bi, s, pt: (bi, 0, 0))
    gs = pltpu.PrefetchScalarGridSpec(
        num_scalar_prefetch=1,
        grid=(db, n_steps),
        in_specs=[qspec, const((rows, 1)), const((sbw, ns)), const((sbw, ns)), const((ns, ns)),
                  const((page, page))] + [page_spec(i) for i in range(pg)] * 2,
        out_specs=qspec,
        scratch_shapes=[pltpu.VMEM((rows, 1), F32), pltpu.VMEM((rows, sbw), F32)],
    )
    return pl.pallas_call(
        functools.partial(_sb_sample_kernel, pg=pg, t=t, h=h, hd=hd),
        grid_spec=gs,
        out_shape=jax.ShapeDtypeStruct((db, t, sbw), BF16),
        compiler_params=_params(("parallel", "arbitrary")),
        name="sb_sample",
    )(page_table.reshape(-1), q, bias_rows, kt_new, vt_new, _neg_tri(ns), _neg_tri(page),
      *([cache_kt] * pg), *([cache_vt] * pg))


def _mix_out_kernel(*refs, seq, period, mode, final_norm, n_exp):
    it = iter(refs)
    x_ref, o_ref, bg_ref, u_ref = next(it), next(it), next(it), next(it)
    if period is None:
        uprev_ref = next(it)
    else:
        h1_ref, h2_ref = next(it), next(it)
    sga_ref, sgb_ref, cw_ref, wa_ref, wb_ref, wo_ref, gf_ref = (next(it) for _ in range(7))
    if mode == "dense":
        wg_ref, wu_ref, wd_ref = next(it), next(it), next(it)
        if final_norm:
            gn_ref = next(it)
        y_ref = next(it)
    else:
        rt_ref, tri_ref = next(it), next(it)
        x1_ref, xn_ref, ri_ref, rg_ref, cnt_ref = (next(it) for _ in range(5))

    tm = x_ref.shape[0]
    u = u_ref[...]
    row = lax.broadcasted_iota(I32, u.shape, 0)
    u1 = pltpu.roll(u, 1, axis=0)
    u2 = pltpu.roll(u, 2, axis=0)
    if period is None:
        first = (pl.program_id(0) % (seq // tm)) == 0
        keep = jnp.where(first, 0.0, 1.0)
        hm1 = uprev_ref[SUBLANES - 1:SUBLANES, :] * keep
        hm2 = uprev_ref[SUBLANES - 2:SUBLANES - 1, :] * keep
        u1 = jnp.where(row == 0, hm1, u1)
        u2 = jnp.where(row == 0, hm2, jnp.where(row == 1, hm1, u2))
    else:
        u1 = jnp.where(row % period == 0, h1_ref[...], u1)
        u2 = jnp.where(row % period < 2, h2_ref[...], u2)
    cv = cw_ref[0:1, :] * u2 + cw_ref[1:2, :] * u1 + cw_ref[2:3, :] * u

    ya = jnp.dot(o_ref[...], wa_ref[...], preferred_element_type=F32)
    yb = jnp.dot((bg_ref[...].astype(F32) * cv).astype(BF16), wb_ref[...], preferred_element_type=F32)
    mix = (sga_ref[...].astype(F32) * ya + sgb_ref[...].astype(F32) * yb).astype(BF16)
    x1 = x_ref[...] + jnp.dot(mix, wo_ref[...], preferred_element_type=F32)
    xnf = _rms(x1, gf_ref[...])
    xn = xnf.astype(BF16)

    if mode == "dense":
        hg = jnp.dot(xn, wg_ref[...], preferred_element_type=F32)
        hu = jnp.dot(xn, wu_ref[...], preferred_element_type=F32)
        hh = (hg * _sigmoid(hg) * hu).astype(BF16)
        y = x1 + jnp.dot(hh, wd_ref[...], preferred_element_type=F32)
        if final_norm:
            y = _rms(y, gn_ref[...])
        y_ref[...] = y
        return

    x1_ref[...] = x1
    xn_ref[...] = xn
    ep = rt_ref.shape[0]
    rt = rt_ref[...]
    rh = rt.astype(BF16)
    rl = (rt - rh.astype(F32)).astype(BF16)
    xl = (xnf - xn.astype(F32)).astype(BF16)
    dn = (((1,), (1,)), ((), ()))
    logits = (lax.dot_general(rh, xn, dn, preferred_element_type=F32)
              + lax.dot_general(rl, xn, dn, preferred_element_type=F32)
              + lax.dot_general(rh, xl, dn, preferred_element_type=F32))
    ie = lax.broadcasted_iota(I32, (ep, tm), 0)
    logits = jnp.where(ie < n_exp, logits, -jnp.inf)
    m1 = jnp.max(logits, axis=0, keepdims=True)
    i1 = jnp.min(jnp.where(logits == m1, ie, ep), axis=0, keepdims=True)
    l2 = jnp.where(ie == i1, -jnp.inf, logits)
    m2 = jnp.max(l2, axis=0, keepdims=True)
    i2 = jnp.min(jnp.where(l2 == m2, ie, ep), axis=0, keepdims=True)
    e2 = jnp.exp(m2 - m1)
    g1 = 1.0 / (1.0 + e2)
    g2 = e2 / (1.0 + e2)
    oh = jnp.where((ie == i1) | (ie == i2), 1.0, 0.0)
    before = jnp.dot(oh.astype(BF16), tri_ref[...], preferred_element_type=F32)
    w1 = jnp.sum(jnp.where(ie == i1, before, 0.0), axis=0, keepdims=True).astype(I32)
    w2 = jnp.sum(jnp.where(ie == i2, before, 0.0), axis=0, keepdims=True).astype(I32)
    r8 = lax.broadcasted_iota(I32, (SUBLANES, tm), 0)
    ri_ref[...] = jnp.where(r8 == 0, i1, jnp.where(r8 == 1, i2, jnp.where(r8 == 2, w1, jnp.where(r8 == 3, w2, 0))))
    rg_ref[...] = jnp.where(r8 == 0, g1, jnp.where(r8 == 1, g2, 0.0))
    cnt_ref[...] = jnp.broadcast_to(jnp.sum(oh, axis=1, keepdims=True), (ep, LANES))


def _mix_out(x, o, bg, u, sga, sgb, cw, wa, wb, wo, gf, *, seq, hist, mode, ffn, final_g, tm):
    n, d = x.shape
    sbw, dc = o.shape[1], bg.shape[1]
    nt = n // tm
    row = lambda c: pl.BlockSpec((tm, c), lambda i: (i, 0))
    ins = [x, o, bg, u]
    specs = [row(d), row(sbw), row(dc), row(dc)]
    if hist is None:
        assert seq % tm == 0 and tm % SUBLANES == 0
        period = None
        ins.append(u)
        specs.append(pl.BlockSpec((SUBLANES, dc), lambda i: (jnp.maximum(i * (tm // SUBLANES) - 1, 0), 0)))
    else:
        assert tm % seq == 0
        period = seq
        ins += list(hist)
        specs += [row(dc), row(dc)]
    cwp = jnp.pad(cw, ((0, SUBLANES - cw.shape[0]), (0, 0)))
    ins += [sga, sgb, cwp, wa, wb, wo, gf]
    specs += [row(d), row(d)] + [_const_spec(a.shape) for a in (cwp, wa, wb, wo, gf)]
    if mode == "dense":
        ins += list(ffn)
        specs += [_const_spec(a.shape) for a in ffn]
        if final_g is not None:
            ins.append(final_g)
            specs.append(_const_spec(final_g.shape))
        out_specs = row(d)
        out_shape = jax.ShapeDtypeStruct((n, d), F32)
        n_exp = 0
    else:
        router = ffn
        n_exp = router.shape[1]
        ep = -(-n_exp // BF16_ROWS) * BF16_ROWS
        rt = jnp.pad(router.T, ((0, ep - n_exp), (0, 0)))
        ti = lax.broadcasted_iota(I32, (tm, tm), 0)
        tj = lax.broadcasted_iota(I32, (tm, tm), 1)
        tri = jnp.where(ti < tj, 1.0, 0.0).astype(BF16)
        ins += [rt, tri]
        specs += [_const_spec(rt.shape), _const_spec(tri.shape)]
        lane_rows = lambda: pl.BlockSpec((SUBLANES, tm), lambda i: (0, i))
        out_specs = [row(d), row(d), lane_rows(), lane_rows(), pl.BlockSpec((None, ep, LANES), lambda i: (i, 0, 0))]
        out_shape = [jax.ShapeDtypeStruct((n, d), F32), jax.ShapeDtypeStruct((n, d), BF16),
                     jax.ShapeDtypeStruct((SUBLANES, n), I32), jax.ShapeDtypeStruct((SUBLANES, n), F32),
                     jax.ShapeDtypeStruct((nt, ep, LANES), F32)]
    return pl.pallas_call(
        functools.partial(_mix_out_kernel, seq=seq, period=period, mode=mode,
                          final_norm=final_g is not None, n_exp=n_exp),
        grid=(nt,),
        in_specs=specs,
        out_specs=out_specs,
        out_shape=out_shape,
        compiler_params=_params(("parallel",)),
        name="mix_out_" + mode,
    )(*ins)


def _windowed(body, r0, r1, size, win):
    half = size // 2
    if half % win or half == 0:
        body(0, size)
        return
    start = jnp.minimum(r0 & (-win), size - half)
    fits = r1 <= start + half

    @pl.when(fits)
    def _():
        body(pl.multiple_of(start, win), half)

    @pl.when(jnp.logical_not(fits))
    def _():
        body(0, size)


def _dispatch_kernel(ii_ref, ij_ref, fl_ref, r0_ref, r1_ref, n_ref, x_ref, pos_ref, rg_ref, *rest, ts, win, n_cast):
    w = pl.program_id(0)
    if n_cast:
        src, (xs_ref, gs_ref), dst = rest[:3], rest[3:5], rest[5:]

        @pl.when(w < n_cast)
        def _():
            for a, b in zip(src, dst):
                b[...] = a[...].astype(BF16)
    else:
        xs_ref, gs_ref = rest

    @pl.when(w < n_ref[0])
    def _():
        tt = x_ref.shape[0]

        @pl.when(fl_ref[w] == 1)
        def _():
            xs_ref[...] = jnp.zeros_like(xs_ref)
            gs_ref[...] = jnp.zeros_like(gs_ref)

        def fill(start, size):
            rows = pl.ds(start, size)
            r = ii_ref[w] * ts + start + lax.broadcasted_iota(I32, (size, tt), 0)
            m1 = r == pos_ref[0:1, :]
            m2 = r == pos_ref[1:2, :]
            sel = jnp.where(m1 | m2, 1.0, 0.0).astype(BF16)
            xs = jnp.dot(sel, x_ref[...], preferred_element_type=F32).astype(BF16)
            gate = jnp.sum(jnp.where(m1, rg_ref[0:1, :], 0.0) + jnp.where(m2, rg_ref[1:2, :], 0.0),
                           axis=1, keepdims=True)
            xs_ref[rows, :] += xs
            gs_ref[rows, :] += jnp.broadcast_to(gate, (size, gs_ref.shape[1]))

        _windowed(fill, r0_ref[w], r1_ref[w], ts, win)


def _dispatch(xn, pos, rg, items, n_items, ts, tt, rows, cast=()):
    n, d = xn.shape
    ii, ij, first, r0, r1 = items
    steps = ii.shape[0]
    win = _pick(ts, (WIN,))
    tok = lambda w, ii, ij, *_: (ij[w], 0)
    tok_t = lambda w, ii, ij, *_: (0, ij[w])
    srt = lambda w, ii, *_: (ii[w], 0)
    in_specs = [pl.BlockSpec((tt, d), tok), pl.BlockSpec((SUBLANES, tt), tok_t), pl.BlockSpec((SUBLANES, tt), tok_t)]
    out_specs = [pl.BlockSpec((ts, d), srt), pl.BlockSpec((ts, LANES), srt)]
    out_shape = [jax.ShapeDtypeStruct((rows, d), BF16), jax.ShapeDtypeStruct((rows, LANES), F32)]
    n_cast = 0
    if cast:
        n_exp, _, dff = cast[0].shape
        ncb = dff // CAST_COLS
        n_cast = n_exp * ncb
        assert dff % CAST_COLS == 0 and n_cast <= steps
        blk = lambda w: jnp.minimum(w, n_cast - 1)
        up = pl.BlockSpec((None, d, CAST_COLS), lambda w, *_: (blk(w) // ncb, 0, blk(w) % ncb))
        down = pl.BlockSpec((None, CAST_COLS, d), lambda w, *_: (blk(w) // ncb, blk(w) % ncb, 0))
        in_specs += [up, up, down]
        out_specs += [up, up, down]
        out_shape += [jax.ShapeDtypeStruct(a.shape, BF16) for a in cast]
    gs = pltpu.PrefetchScalarGridSpec(num_scalar_prefetch=6, grid=(steps,), in_specs=in_specs, out_specs=out_specs)
    res = pl.pallas_call(
        functools.partial(_dispatch_kernel, ts=ts, win=win, n_cast=n_cast),
        grid_spec=gs,
        out_shape=out_shape,
        compiler_params=_params(("arbitrary",)),
        name="moe_dispatch",
    )(ii, ij, first, r0, r1, n_items, xn, pos, rg, *cast)
    return res[0], res[1], tuple(res[2:])


def _experts_kernel(te_ref, nv_ref, xs_ref, wg_ref, wu_ref, wd_ref, gs_ref, y_ref, acc_ref):
    i = pl.program_id(0)
    c = pl.program_id(1)

    @pl.when(i < nv_ref[0])
    def _():
        x = xs_ref[...]
        hg = jnp.dot(x, wg_ref[...], preferred_element_type=F32)
        hu = jnp.dot(x, wu_ref[...], preferred_element_type=F32)
        hh = (hg * _sigmoid(hg) * hu).astype(BF16)
        part = jnp.dot(hh, wd_ref[...], preferred_element_type=F32)

        @pl.when(c == 0)
        def _():
            acc_ref[...] = part

        @pl.when(c > 0)
        def _():
            acc_ref[...] += part

        @pl.when(c == pl.num_programs(1) - 1)
        def _():
            y_ref[...] = (acc_ref[...] * gs_ref[:, 0:1]).astype(BF16)


def _experts(xs, gsort, wg, wu, wd, tile_expert, n_valid, ts):
    rows, d = xs.shape
    dff = wg.shape[2]
    fc = _pick(dff, (2048, 1792, 1536, 1280, 1024, 896, 768, 640, 512, 384, 256, 128))
    nfc = dff // fc
    tile = lambda i, nv: jnp.maximum(jnp.minimum(i, nv[0] - 1), 0)
    chunk = lambda i, c, nv: jnp.where(i < nv[0], c, nfc - 1)
    gs = pltpu.PrefetchScalarGridSpec(
        num_scalar_prefetch=2,
        grid=(rows // ts, nfc),
        in_specs=[pl.BlockSpec((ts, d), lambda i, c, te, nv: (tile(i, nv), 0)),
                  pl.BlockSpec((None, d, fc), lambda i, c, te, nv: (te[i], 0, chunk(i, c, nv))),
                  pl.BlockSpec((None, d, fc), lambda i, c, te, nv: (te[i], 0, chunk(i, c, nv))),
                  pl.BlockSpec((None, fc, d), lambda i, c, te, nv: (te[i], chunk(i, c, nv), 0)),
                  pl.BlockSpec((ts, LANES), lambda i, c, te, nv: (tile(i, nv), 0))],
        out_specs=pl.BlockSpec((ts, d), lambda i, c, te, nv: (tile(i, nv), 0)),
        scratch_shapes=[pltpu.VMEM((ts, d), F32)],
    )
    return pl.pallas_call(
        _experts_kernel,
        grid_spec=gs,
        out_shape=jax.ShapeDtypeStruct((rows, d), BF16),
        compiler_params=_params(("arbitrary", "arbitrary")),
        name="moe_experts",
    )(tile_expert, n_valid, xs, wg, wu, wd, gsort)


def _combine_kernel(ii_ref, ij_ref, fl_ref, r0_ref, r1_ref, n_ref, x1_ref, ys_ref, posc_ref, gn_ref, y_ref,
                    *, ts, win, final_norm):
    w = pl.program_id(0)

    @pl.when(w < n_ref[0])
    def _():
        tt = x1_ref.shape[0]
        fl = fl_ref[w]

        @pl.when(fl % 2 == 1)
        def _():
            y_ref[...] = x1_ref[...]

        def gather(start, size):
            r = ii_ref[w] * ts + start + lax.broadcasted_iota(I32, (tt, size), 1)
            sel = jnp.where((r == posc_ref[:, 0:1]) | (r == posc_ref[:, 1:2]), 1.0, 0.0).astype(BF16)
            y_ref[...] += jnp.dot(sel, ys_ref[pl.ds(start, size), :], preferred_element_type=F32)

        _windowed(gather, r0_ref[w], r1_ref[w], ts, win)

        if final_norm:
            @pl.when(fl >= 2)
            def _():
                y_ref[...] = _rms(y_ref[...], gn_ref[...])


def _combine(x1, ys, posc, items, n_items, final_g, ts, tt):
    n, d = x1.shape
    ii, ij, flags, r0, r1 = items
    win = _pick(ts, (MXU_DIM,))
    gn = final_g if final_g is not None else jnp.ones((1, d), F32)
    tok = lambda w, ii, ij, *_: (ij[w], 0)
    gs = pltpu.PrefetchScalarGridSpec(
        num_scalar_prefetch=6,
        grid=(ii.shape[0],),
        in_specs=[pl.BlockSpec((tt, d), tok),
                  pl.BlockSpec((ts, d), lambda w, ii, *_: (ii[w], 0)),
                  pl.BlockSpec((tt, LANES), tok),
                  pl.BlockSpec((1, d), lambda w, *_: (0, 0))],
        out_specs=pl.BlockSpec((tt, d), tok),
    )
    return pl.pallas_call(
        functools.partial(_combine_kernel, ts=ts, win=win, final_norm=final_g is not None),
        grid_spec=gs,
        out_shape=jax.ShapeDtypeStruct((n, d), F32),
        compiler_params=_params(("arbitrary",)),
        name="moe_combine",
    )(ii, ij, flags, r0, r1, n_items, x1, ys, posc, gn)


def _route_tables(ri, cnt, n_exp, tt, ts, g):
    n = ri.shape[1]
    nj = n // tt
    cnt = cnt[:, :n_exp, 0].astype(I32)
    tot = jnp.sum(cnt, axis=0)
    gsize = (tot + ts - 1) // ts * ts
    gend = jnp.cumsum(gsize)
    goff = gend - gsize
    seg_start = goff[None, :] + jnp.cumsum(cnt, axis=0) - cnt
    e1, e2, w1, w2 = ri[0], ri[1], ri[2], ri[3]
    oh = lambda e: e[:, None] == jnp.arange(n_exp, dtype=I32)[None, :]
    base = jnp.repeat(seg_start, tt, axis=0)
    pos1 = jnp.sum(jnp.where(oh(e1), base, 0), axis=1) + w1
    pos2 = jnp.sum(jnp.where(oh(e2), base, 0), axis=1) + w2
    two = lambda k, a, b: jnp.where(k == 0, a, jnp.where(k == 1, b, 0))
    pos = two(jnp.arange(SUBLANES, dtype=I32)[:, None], pos1[None, :], pos2[None, :])
    posc = two(jnp.arange(LANES, dtype=I32)[None, :], pos1[:, None], pos2[:, None])

    rows = -(-(n * TOP_K + n_exp * ts) // g) * g
    gm = g // tt
    njd, nid = nj // gm, rows // g
    s_lo = seg_start.reshape(njd, gm, n_exp)[:, 0]
    s_hi = s_lo + jnp.sum(cnt.reshape(njd, gm, n_exp), axis=1)
    t0 = (jnp.arange(nid, dtype=I32) * g)[:, None, None]
    lo = jnp.maximum(s_lo[None], t0)
    hi = jnp.minimum(s_hi[None], t0 + g)
    hit = hi > lo
    it_v = jnp.any(hit, axis=2)
    it_r0 = jnp.min(jnp.where(hit, lo - t0, g), axis=2)
    it_r1 = jnp.max(jnp.where(hit, hi - t0, 0), axis=2)
    n_items = jnp.sum(it_v).astype(I32).reshape(1)
    w_max = min(nid * njd, n_exp * njd + nid)
    assert nid * njd < 2 ** (31 - 2 * ROW_BITS) and g <= 2 ** ROW_BITS

    def ordered(major, minor, n_minor):
        code = ((major * n_minor + minor) << (2 * ROW_BITS)) | (it_r0 << ROW_BITS) | (it_r1 - 1)
        code = jnp.sort(jnp.where(it_v, code, jnp.iinfo(jnp.int32).max).reshape(-1))[:w_max]
        live = jnp.arange(w_max) < n_items[0]
        code = jnp.where(live, code, code[jnp.maximum(n_items[0] - 1, 0)])
        pair = code >> (2 * ROW_BITS)
        mask = (1 << ROW_BITS) - 1
        return pair // n_minor, pair % n_minor, live, (code >> ROW_BITS) & mask, (code & mask) + 1

    ii = jnp.arange(nid, dtype=I32)[:, None]
    jj = jnp.arange(njd, dtype=I32)[None, :]
    si, sj, live, r0, r1 = ordered(ii, jj, njd)
    first = jnp.concatenate([jnp.ones((1,), bool), si[1:] != si[:-1]]) & live
    disp = (si, sj, first.astype(I32), r0, r1)
    cj, ci, live, r0, r1 = ordered(jj, ii, nid)
    cfirst = jnp.concatenate([jnp.ones((1,), bool), cj[1:] != cj[:-1]]) & live
    nxt_live = jnp.concatenate([live[1:], jnp.zeros((1,), bool)])
    clast = (jnp.concatenate([cj[1:] != cj[:-1], jnp.ones((1,), bool)]) | ~nxt_live) & live
    comb = (ci, cj, cfirst.astype(I32) + 2 * clast.astype(I32), r0, r1)

    ni = rows // ts
    n_valid = (gend[-1] // ts).astype(I32).reshape(1)
    tstart = jnp.arange(ni, dtype=I32) * ts
    te = jnp.sum(tstart[:, None] >= gend[None, :], axis=1).astype(I32)
    te_last = jnp.sum((n_valid[0] - 1) * ts >= gend).astype(I32)
    te = jnp.where(jnp.arange(ni) < n_valid[0], jnp.minimum(te, n_exp - 1), te_last)
    return pos, posc, disp, comb, n_items, te, n_valid, rows


def _moe(x1, xn, ri, rg, cnt, wg, wu, wd, final_g, tt):
    n_exp = wg.shape[0]
    n = x1.shape[0]
    ts = tt
    g = tt * 2 if (n // tt) % 2 == 0 and tt * 2 <= MOE_COARSE else tt
    pos, posc, disp, comb, n_items, te, n_valid, rows = _route_tables(ri, cnt, n_exp, tt, ts, g)
    cast = (wg, wu, wd) if wg.dtype == F32 else ()
    xs, gsort, cast = _dispatch(xn, pos, rg, disp, n_items, g, g, rows, cast)
    if cast:
        wg, wu, wd = cast
    ys = _experts(xs, gsort, wg, wu, wd, te, n_valid, ts)
    return _combine(x1, ys, posc, comb, n_items, final_g, g, g), (wg, wu, wd)


def kernel(x_prompt, x_sample, cache_k, cache_v, state_conv, page_table, norm_mix, w_in, sb_bias, w_a,
           conv_w, w_b, w_o, norm_ffn, ffn_wg, ffn_wu, ffn_wd, router, moe_wg, moe_wu, moe_wd, norm_final):
    b, s, d = x_prompt.shape
    db, t, _ = x_sample.shape
    depth = w_in.shape[0]
    sbw = w_a.shape[1]
    dc = w_b.shape[1]
    h = sb_bias.shape[1]
    hd = sbw // h
    scale = hd ** -0.5 * LOG2E
    ckt = jnp.transpose(cache_k, (0, 1, 3, 4, 2))
    cvt = jnp.transpose(cache_v, (0, 1, 3, 4, 2))
    row2 = lambda g: g.reshape(1, d)

    xp = x_prompt.reshape(b * s, d)
    xs = x_sample.reshape(db * t, d)
    tm_p = _pick(s, (512, 256, 128))
    tm_s = db * t
    cp_l, ks_l, vs_l, cs_l = [], [], [], []
    kv_all = None
    trow = jnp.arange(db * t) % t
    for l in range(depth):
        win = w_in[l].astype(BF16)
        wa, wb, wo = w_a[l].astype(BF16), w_b[l].astype(BF16), w_o[l].astype(BF16)
        last = l == depth - 1
        final_g = row2(norm_final) if last else None
        fi = l // 2
        if l % 2 == 0:
            mode = "dense"
            ffn = (ffn_wg[fi].astype(BF16), ffn_wu[fi].astype(BF16), ffn_wd[fi].astype(BF16))
        else:
            mode = "moe"
            ffn = router[fi]
            experts = (moe_wg[fi], moe_wu[fi], moe_wd[fi])

        q, kt_all, vt_all, ktb, vtb, bg, u, sga, sgb = _in_proj(
            xp, row2(norm_mix[l]), win, b, s, sbw, dc, scale, layer=l, depth=depth, kv_all=kv_all)
        kv_all = (kt_all, vt_all)
        o = _sb_prompt(q, ktb, vtb, sb_bias[l], hd)
        res = _mix_out(xp, o, bg, u, sga, sgb, conv_w[l], wa, wb, wo, row2(norm_ffn[l]), seq=s, hist=None,
                       mode=mode, ffn=ffn, final_g=final_g if mode == "dense" else None, tm=tm_p)
        if mode == "dense":
            xp = res
        else:
            xp, experts = _moe(*res, *experts, final_g, tm_p)
        cp_l.append(u.reshape(b, s, dc)[:, s - (conv_w.shape[1] - 1):])

        q, kt, vt, _, _, bg, u, sga, sgb = _in_proj(xs, row2(norm_mix[l]), win, 1, db * t, sbw, dc, scale)
        kt, vt = kt[0], vt[0]
        o = _sb_sample(q.reshape(db, t, sbw), kt[0], vt[0], sb_bias[l], ckt, cvt, l, page_table, hd)
        st = state_conv[l]
        h1 = jnp.repeat(st[:, 1], t, axis=0)
        h2 = jnp.where((trow == 0)[:, None], jnp.repeat(st[:, 0], t, axis=0), h1)
        res = _mix_out(xs, o.reshape(db * t, sbw), bg, u, sga, sgb, conv_w[l], wa, wb, wo, row2(norm_ffn[l]),
                       seq=t, hist=(h1, h2), mode=mode, ffn=ffn,
                       final_g=final_g if mode == "dense" else None, tm=tm_s)
        xs = res if mode == "dense" else _moe(*res, *experts, final_g, tm_s)[0]
        ks_l.append(kt[0].T.reshape(db, t, h, hd))
        vs_l.append(vt[0].T.reshape(db, t, h, hd))
        cs_l.append(jnp.concatenate([st, u.reshape(db, t, dc)], axis=1)[:, t:])

    seq_major = lambda a: jnp.transpose(a.reshape(depth, b, h, hd, s), (0, 1, 4, 2, 3))
    return (xp.reshape(b, s, d), xs.reshape(db, t, d), seq_major(kv_all[0]), seq_major(kv_all[1]), jnp.stack(cp_l),
            jnp.stack(ks_l), jnp.stack(vs_l), jnp.stack(cs_l))
```
